```python
import jax, jax.numpy as jnp
from jax import lax
import numpy as np

D_MODEL = 2048
BATCH = 2
SEQ = 8192
DEPTH = 1

N_META = 16
CONV_CH = D_MODEL
CONV_WIDTH = 31
GLA_HEADS = 4
GLA_DK = D_MODEL // 2 // GLA_HEADS
GLA_DV = D_MODEL // GLA_HEADS
GLA_GATE_RANK = 16
GLA_TAU = 16.0
GLA_CHUNK = 64
PEER_HEADS = 8
PEER_NKEYS = 128
PEER_N = PEER_NKEYS * PEER_NKEYS
PEER_DK = 256
PEER_TOPK = 16
PEER_TOKEN_BLOCK = 128
EPS = 1e-6

COL_CONV = 2 * CONV_CH
COL_Q = GLA_HEADS * GLA_DK
COL_K = GLA_HEADS * GLA_DK
COL_V = GLA_HEADS * GLA_DV
COL_R = GLA_HEADS * GLA_DV
COL_A = GLA_GATE_RANK
COL_MERGE = 2 * D_MODEL
COL_TOTAL = COL_CONV + COL_Q + COL_K + COL_V + COL_R + COL_A + COL_MERGE

kernel_name = "hybrid_conv_gla_peer_block"


def _rmsnorm(x, g):
    xf = x.astype(jnp.float32)
    y = xf * lax.rsqrt(jnp.mean(xf * xf, axis=-1, keepdims=True) + EPS)
    return (y * g.astype(jnp.float32)).astype(x.dtype)


def _conv_module(a, conv_w, conv_b, ln_g, ln_b, w_out):
    h = a[..., :CONV_CH] * jax.nn.sigmoid(a[..., CONV_CH:])
    hp = jnp.pad(h, ((0, 0), (CONV_WIDTH - 1, 0), (0, 0)))
    h = lax.conv_general_dilated(hp, conv_w[:, None, :], window_strides=(1,), padding='VALID',
                                 dimension_numbers=('NWC', 'WIO', 'NWC'),
                                 feature_group_count=CONV_CH) + conv_b
    hf = h.astype(jnp.float32)
    mu = jnp.mean(hf, axis=-1, keepdims=True)
    var = jnp.mean(jnp.square(hf - mu), axis=-1, keepdims=True)
    hf = (hf - mu) * lax.rsqrt(var + EPS) * ln_g.astype(jnp.float32) + ln_b.astype(jnp.float32)
    h = jax.nn.silu(hf).astype(a.dtype)
    return h @ w_out


def _gla_branch(q, k, v, r, a_low, w_a2, b_a, norm_g, w_o):
    B, L, _ = q.shape
    H, dk, dv, C = GLA_HEADS, GLA_DK, GLA_DV, GLA_CHUNK
    pad = C - N_META
    log_a = jax.nn.log_sigmoid((a_low @ w_a2 + b_a).astype(jnp.float32)) / GLA_TAU

    def to_chunks(t, d):
        t = jnp.pad(t.astype(jnp.float32), ((0, 0), (pad, 0), (0, 0)))
        n = t.shape[1] // C
        return t.reshape(B, n, C, H, d).transpose(1, 0, 3, 2, 4)

    qc = to_chunks(q, dk) * (dk ** -0.5)
    kc = to_chunks(k, dk)
    vc = to_chunks(v, dv)
    gc = to_chunks(log_a, dk)
    mask = jnp.tril(jnp.ones((C, C), dtype=bool))[None, None, :, :, None]

    def step(S, inp):
        qb, kb, vb, gb = inp
        b = jnp.cumsum(gb, axis=2)
        o_inter = jnp.einsum('bhcd,bhde->bhce', qb * jnp.exp(b), S)
        diff = b[:, :, :, None, :] - b[:, :, None, :, :]
        decay = jnp.exp(jnp.where(mask, diff, -jnp.inf))
        att = jnp.einsum('bhid,bhjd,bhijd->bhij', qb, kb, decay)
        o_intra = jnp.einsum('bhij,bhje->bhie', att, vb)
        b_last = b[:, :, -1:, :]
        S_new = jnp.exp(b_last[:, :, 0, :])[..., None] * S + \
            jnp.einsum('bhjd,bhje->bhde', kb * jnp.exp(b_last - b), vb)
        return S_new, o_inter + o_intra

    S0 = jnp.zeros((B, H, dk, dv), jnp.float32)
    _, o = lax.scan(step, S0, (qc, kc, vc, gc))
    n = o.shape[0]
    o = o.transpose(1, 0, 3, 2, 4).reshape(B, n * C, H, dv)[:, pad:]
    o = o * lax.rsqrt(jnp.mean(o * o, axis=-1, keepdims=True) + EPS)
    o = o * norm_g.astype(jnp.float32).reshape(H, dv)
    o = o * jax.nn.silu(r.astype(jnp.float32)).reshape(B, L, H, dv)
    return o.reshape(B, L, H * dv).astype(w_o.dtype) @ w_o


def _peer(xn, wq, k1, k2, U, V):
    B, L, D = xn.shape
    PH, NK, K, half = PEER_HEADS, PEER_NKEYS, PEER_TOPK, PEER_DK // 2
    q = (xn @ wq).astype(jnp.float32).reshape(B, L, PH, PEER_DK)
    s1 = jnp.einsum('blhd,hnd->blhn', q[..., :half], k1.astype(jnp.float32))
    s2 = jnp.einsum('blhd,hnd->blhn', q[..., half:], k2.astype(jnp.float32))
    v1, i1 = lax.top_k(s1, K)
    v2, i2 = lax.top_k(s2, K)
    cand = (v1[..., :, None] + v2[..., None, :]).reshape(B, L, PH, K * K)
    cidx = (i1[..., :, None] * NK + i2[..., None, :]).reshape(B, L, PH, K * K)
    top_s, pos = lax.top_k(cand, K)
    idx = jnp.take_along_axis(cidx, pos, axis=-1)
    gate = jax.nn.softmax(top_s, axis=-1)

    T = B * L
    TB = PEER_TOKEN_BLOCK
    nblk = -(-T // TB)
    padT = nblk * TB - T
    xt = jnp.pad(xn.reshape(T, D), ((0, padT), (0, 0))).reshape(nblk, TB, D)
    it = jnp.pad(idx.reshape(T, PH, K), ((0, padT), (0, 0), (0, 0))).reshape(nblk, TB, PH, K)
    gt = jnp.pad(gate.reshape(T, PH, K), ((0, padT), (0, 0), (0, 0))).reshape(nblk, TB, PH, K)

    def block(args):
        xb, ib, gb = args
        u = U[ib]
        h = jax.nn.gelu(jnp.einsum('thkd,td->thk', u, xb).astype(jnp.float32), approximate=False)
        w = (gb * h).astype(xb.dtype)
        return jnp.einsum('thk,thkd->td', w, V[ib])

    y = lax.map(block, (xt, it, gt)).reshape(nblk * TB, D)[:T]
    return y.reshape(B, L, D).astype(xn.dtype)


def setup_inputs(seed: int = 0) -> dict:
    key = jax.random.key(seed)
    ks = jax.random.split(key, 24)

    def nrm(k, shape, scale):
        return jax.random.normal(k, shape, jnp.float32) * scale

    Dp = DEPTH
    return {
        "x": nrm(ks[0], (BATCH, SEQ, D_MODEL), 1.0),
        "meta_tokens": nrm(ks[1], (N_META, D_MODEL), 1.0),
        "norm1_g": 1.0 + nrm(ks[2], (Dp, D_MODEL), 0.02),
        "w_in": nrm(ks[3], (Dp, D_MODEL, COL_TOTAL), D_MODEL ** -0.5),
        "w_alpha2": nrm(ks[4], (Dp, GLA_GATE_RANK, GLA_HEADS * GLA_DK), GLA_GATE_RANK ** -0.5),
        "b_alpha": nrm(ks[5], (Dp, GLA_HEADS * GLA_DK), 0.1),
        "conv_w": nrm(ks[6], (Dp, CONV_WIDTH, CONV_CH), CONV_WIDTH ** -0.5),
        "conv_b": nrm(ks[7], (Dp, CONV_CH), 0.02),
        "conv_ln_g": 1.0 + nrm(ks[8], (Dp, CONV_CH), 0.02),
        "conv_ln_b": nrm(ks[9], (Dp, CONV_CH), 0.02),
        "w_conv_out": nrm(ks[10], (Dp, CONV_CH, D_MODEL), CONV_CH ** -0.5),
        "gla_norm_g": 1.0 + nrm(ks[11], (Dp, GLA_HEADS * GLA_DV), 0.02),
        "w_gla_out": nrm(ks[12], (Dp, GLA_HEADS * GLA_DV, D_MODEL), (GLA_HEADS * GLA_DV) ** -0.5),
        "w_out": nrm(ks[13], (Dp, D_MODEL, D_MODEL), D_MODEL ** -0.5),
        "norm2_g": 1.0 + nrm(ks[14], (Dp, D_MODEL), 0.02),
        "peer_wq": nrm(ks[15], (Dp, D_MODEL, PEER_HEADS * PEER_DK), D_MODEL ** -0.5),
        "peer_k1": nrm(ks[16], (Dp, PEER_HEADS, PEER_NKEYS, PEER_DK // 2), (PEER_DK // 2) ** -0.5),
        "peer_k2": nrm(ks[17], (Dp, PEER_HEADS, PEER_NKEYS, PEER_DK // 2), (PEER_DK // 2) ** -0.5),
        "peer_u": nrm(ks[18], (Dp, PEER_N, D_MODEL), D_MODEL ** -0.5),
        "peer_v": nrm(ks[19], (Dp, PEER_N, D_MODEL), PEER_HEADS ** -0.5),
        "normf_g": 1.0 + nrm(ks[20], (D_MODEL,), 0.02),
    }


def reference(x, meta_tokens, norm1_g, w_in, w_alpha2, b_alpha, conv_w, conv_b, conv_ln_g, conv_ln_b,
              w_conv_out, gla_norm_g, w_gla_out, w_out, norm2_g, peer_wq, peer_k1, peer_k2,
              peer_u, peer_v, normf_g):
    B = x.shape[0]
    meta = jnp.broadcast_to(meta_tokens[None].astype(x.dtype), (B, N_META, D_MODEL))
    h = jnp.concatenate([meta, x], axis=1)
    offs = [COL_CONV, COL_Q, COL_K, COL_V, COL_R, COL_A]
    cuts = [int(c) for c in np.cumsum(offs)]
    for l in range(DEPTH):
        xn = _rmsnorm(h, norm1_g[l])
        proj = xn @ w_in[l]
        a_conv, q, k, v, r, a_low, g_merge = jnp.split(proj, cuts, axis=-1)
        y_conv = _conv_module(a_conv, conv_w[l], conv_b[l], conv_ln_g[l], conv_ln_b[l], w_conv_out[l])
        y_gla = _gla_branch(q, k, v, r, a_low, w_alpha2[l], b_alpha[l], gla_norm_g[l], w_gla_out[l])
        gates = jax.nn.sigmoid(g_merge.astype(jnp.float32))
        mixed = gates[..., :D_MODEL] * y_conv.astype(jnp.float32) + gates[..., D_MODEL:] * y_gla.astype(jnp.float32)
        h = h + mixed.astype(h.dtype) @ w_out[l]
        h = h + _peer(_rmsnorm(h, norm2_g[l]), peer_wq[l], peer_k1[l], peer_k2[l], peer_u[l], peer_v[l])
    return _rmsnorm(h, normf_g)[:, N_META:]
```

```python
import functools

import jax
import jax.numpy as jnp
from jax import lax
from jax.experimental import pallas as pl
from jax.experimental.pallas import tpu as pltpu

F32 = jnp.float32
BF16 = jnp.bfloat16

N_META = 16
CONV_WIDTH = 31
GLA_HEADS = 4
GLA_TAU = 16.0
GLA_CHUNK = 64
PEER_HEADS = 8
PEER_NKEYS = 128
PEER_TOPK = 16
EPS = 1e-6

LANES = 128
META_TILE = 256
CONV_HALO = 32
VMEM_LIMIT_BYTES = 56 * 1024 * 1024

_NT = (((1,), (1,)), ((), ()))
_TN = (((0,), (0,)), ((), ()))


def _sigmoid(x):
    return 1.0 / (1.0 + jnp.exp(-x))


def _params(*sem):
    return pltpu.CompilerParams(dimension_semantics=sem, vmem_limit_bytes=VMEM_LIMIT_BYTES)


def _inproj_kernel(x_ref, g_ref, w_ref, wa_ref, o_ref, a_ref, xn_ref):
    @pl.when(pl.program_id(1) == 0)
    def _():
        rows = 128
        def body(r, _):
            r0 = pl.multiple_of(r * rows, rows)
            x = x_ref[pl.ds(r0, rows), :]
            ms = jnp.mean(x * x, axis=-1, keepdims=True)
            xn_ref[pl.ds(r0, rows), :] = (x * lax.rsqrt(ms + EPS) * g_ref[...]).astype(BF16)
            return 0
        lax.fori_loop(0, x_ref.shape[0] // rows, body, 0)
        a_ref[...] = jnp.dot(xn_ref[...], wa_ref[...], preferred_element_type=F32)

    o_ref[...] = jnp.dot(xn_ref[...], w_ref[...], preferred_element_type=F32)


def _in_proj(x2d, g, w_main, w_a, tm, tn):
    t, d = x2d.shape
    n = w_main.shape[1]
    return pl.pallas_call(
        _inproj_kernel,
        grid=(t // tm, n // tn),
        in_specs=[
            pl.BlockSpec((tm, d), lambda i, j: (i, 0)),
            pl.BlockSpec((1, d), lambda i, j: (0, 0)),
            pl.BlockSpec((d, tn), lambda i, j: (0, j)),
            pl.BlockSpec((d, LANES), lambda i, j: (0, 0)),
        ],
        out_specs=[
            pl.BlockSpec((tm, tn), lambda i, j: (i, j)),
            pl.BlockSpec((tm, LANES), lambda i, j: (i, 0)),
        ],
        out_shape=[jax.ShapeDtypeStruct((t, n), F32), jax.ShapeDtypeStruct((t, LANES), F32)],
        scratch_shapes=[pltpu.VMEM((tm, d), BF16)],
        compiler_params=_params("parallel", "arbitrary"),
        name="in_proj",
    )(x2d, g, w_main, w_a)


def _conv_kernel(a1_ref, a2_ref, p1_ref, p2_ref, m1_ref, m2_ref, cw_ref, cb_ref, lg_ref, lb_ref,
                 wo_ref, gm_ref, o_ref, hbuf_ref, cbuf_ref, hn_ref):
    tc, ch = a1_ref.shape
    first_tile = pl.program_id(1) == 0

    for r0 in range(0, tc, 64):
        hbuf_ref[CONV_HALO + r0:CONV_HALO + r0 + 64, :] = (
            a1_ref[r0:r0 + 64, :] * _sigmoid(a2_ref[r0:r0 + 64, :]))

    @pl.when(first_tile)
    def _():
        hbuf_ref[0:CONV_HALO, :] = m1_ref[...] * _sigmoid(m2_ref[...])

    @pl.when(jnp.logical_not(first_tile))
    def _():
        hbuf_ref[0:CONV_HALO, :] = p1_ref[...] * _sigmoid(p2_ref[...])

    base = CONV_HALO - (CONV_WIDTH - 1)
    for c0 in range(0, ch, LANES):
        acc = jnp.zeros((tc, LANES), F32) + cb_ref[:, c0:c0 + LANES]
        for j in range(CONV_WIDTH):
            acc = acc + hbuf_ref[base + j:base + j + tc, c0:c0 + LANES] * cw_ref[j:j + 1, c0:c0 + LANES]
        cbuf_ref[:, c0:c0 + LANES] = acc

    rows = 32
    def ln_body(r, _):
        r0 = pl.multiple_of(r * rows, rows)
        x = cbuf_ref[pl.ds(r0, rows), :]
        mu = jnp.mean(x, axis=-1, keepdims=True)
        xc = x - mu
        var = jnp.mean(xc * xc, axis=-1, keepdims=True)
        y = xc * lax.rsqrt(var + EPS) * lg_ref[...] + lb_ref[...]
        hn_ref[pl.ds(r0, rows), :] = (y * _sigmoid(y)).astype(BF16)
        return 0
    lax.fori_loop(0, tc // rows, ln_body, 0)

    y = jnp.dot(hn_ref[...], wo_ref[...], preferred_element_type=F32)
    o_ref[...] = y * _sigmoid(gm_ref[...])


def _conv_module(proj, proj_m, conv_w, conv_b, ln_g, ln_b, w_out, nb, ls, ch, tc):
    nt = ls // tc
    hpt = tc // CONV_HALO
    meta_blk = META_TILE // CONV_HALO - 1
    merge_blk = 10240 // ch

    def cur(col):
        return lambda b, i: (b * nt + i, col)

    def prev(col):
        return lambda b, i: (jnp.maximum((b * nt + i) * hpt - 1, 0), col)

    const = lambda b, i: (0, 0)
    return pl.pallas_call(
        _conv_kernel,
        grid=(nb, nt),
        in_specs=[
            pl.BlockSpec((tc, ch), cur(0)),
            pl.BlockSpec((tc, ch), cur(1)),
            pl.BlockSpec((CONV_HALO, ch), prev(0)),
            pl.BlockSpec((CONV_HALO, ch), prev(1)),
            pl.BlockSpec((CONV_HALO, ch), lambda b, i: (meta_blk, 0)),
            pl.BlockSpec((CONV_HALO, ch), lambda b, i: (meta_blk, 1)),
            pl.BlockSpec((CONV_HALO, ch), const),
            pl.BlockSpec((1, ch), const),
            pl.BlockSpec((1, ch), const),
            pl.BlockSpec((1, ch), const),
            pl.BlockSpec((ch, ch), const),
            pl.BlockSpec((tc, ch), cur(merge_blk)),
        ],
        out_specs=pl.BlockSpec((tc, ch), cur(0)),
        out_shape=jax.ShapeDtypeStruct((nb * ls, ch), F32),
        scratch_shapes=[
            pltpu.VMEM((CONV_HALO + tc, ch), F32),
            pltpu.VMEM((tc, ch), F32),
            pltpu.VMEM((tc, ch), BF16),
        ],
        compiler_params=_params("parallel", "arbitrary"),
        name="conv_module",
    )(proj, proj, proj, proj, proj_m, proj_m, conv_w, conv_b, ln_g, ln_b, w_out, proj)


def _gla_kernel(q_ref, k_ref, v_ref, r_ref, gm_ref, a_ref, km_ref, vm_ref, am_ref, wa2_ref, ba_ref,
                ng_ref, wo_ref, o_ref, st_ref, ob_ref):
    nh = GLA_HEADS
    c = GLA_CHUNK
    tb = q_ref.shape[0]
    dk = q_ref.shape[1] // nh
    dv = v_ref.shape[1] // nh
    scale = dk ** -0.5

    row = lax.broadcasted_iota(jnp.int32, (c, c), 0)
    col = lax.broadcasted_iota(jnp.int32, (c, c), 1)
    tri = (col <= row).astype(F32)

    def log_decay(a_c):
        z = jnp.dot(a_c.astype(BF16), wa2_ref[...], preferred_element_type=F32) + ba_ref[...]
        return (jnp.minimum(z, 0.0) - jnp.log1p(jnp.exp(-jnp.abs(z)))) * (1.0 / GLA_TAU)

    def cumsum_rows(g):
        return jnp.dot(tri, g, preferred_element_type=F32, precision=lax.Precision.HIGHEST)

    @pl.when(pl.program_id(1) == 0)
    def _():
        g = log_decay(am_ref[...])
        ridx = lax.broadcasted_iota(jnp.int32, (c, 1), 0)
        g = jnp.where(ridx >= c - N_META, g, 0.0)
        b = cumsum_rows(g)
        bl = b[c - 1:c, :]
        for h in range(nh):
            ks = slice(h * dk, (h + 1) * dk)
            ke = km_ref[:, ks] * jnp.exp(bl[:, ks] - b[:, ks])
            vh = vm_ref[:, h * dv:(h + 1) * dv]
            st_ref[h] = lax.dot_general(vh.astype(BF16), ke.astype(BF16), _TN, preferred_element_type=F32)

    levels = (32, 16, 8)
    level_masks = []
    for s in levels:
        level_masks.append((row // (2 * s) == col // (2 * s)) & ((row // s) % 2 == 1) & ((col // s) % 2 == 0))
    sub = lax.broadcasted_iota(jnp.int32, (8, c), 0)
    lane = lax.broadcasted_iota(jnp.int32, (8, c), 1)

    def chunk_body(ci, carry):
        r0 = pl.multiple_of(ci * c, c)
        g_all = log_decay(a_ref[pl.ds(r0, c), :])
        b_all = cumsum_rows(g_all)
        for h in range(nh):
            ks = slice(h * dk, (h + 1) * dk)
            vs = slice(h * dv, (h + 1) * dv)
            b = b_all[:, ks]
            q = q_ref[pl.ds(r0, c), ks] * scale
            k = k_ref[pl.ds(r0, c), ks]
            vb = v_ref[pl.ds(r0, c), vs].astype(BF16)
            bl = b[c - 1:c, :]
            st = st_ref[h]

            o = lax.dot_general((q * jnp.exp(b)).astype(BF16), st.astype(BF16), _NT,
                                preferred_element_type=F32)

            att = jnp.zeros((c, c), F32)
            for s, msk in zip(levels, level_masks):
                pieces = []
                for p in range(c // (2 * s)):
                    rr = p * 2 * s + s - 1
                    pieces.append(jnp.broadcast_to(b[rr:rr + 1, :], (2 * s, dk)))
                bref = pieces[0] if len(pieces) == 1 else jnp.concatenate(pieces, axis=0)
                qs = q * jnp.exp(jnp.minimum(b - bref, 0.0))
                kk = k * jnp.exp(jnp.minimum(bref - b, 0.0))
                pm = lax.dot_general(qs.astype(BF16), kk.astype(BF16), _NT, preferred_element_type=F32)
                att = att + jnp.where(msk, pm, 0.0)

            blocks = []
            for bi in range(c // 8):
                qb = q[bi * 8:(bi + 1) * 8, :]
                bb = b[bi * 8:(bi + 1) * 8, :]
                blk = jnp.zeros((8, c), F32)
                for jj in range(8):
                    j = bi * 8 + jj
                    e = jnp.exp(jnp.minimum(bb - b[j:j + 1, :], 0.0))
                    colv = jnp.sum(qb * (k[j:j + 1, :] * e), axis=-1, keepdims=True)
                    blk = blk + jnp.where((lane == j) & (sub >= jj), colv, 0.0)
                blocks.append(blk)
            att = att + jnp.concatenate(blocks, axis=0)

            o = o + jnp.dot(att.astype(BF16), vb, preferred_element_type=F32)

            ke = k * jnp.exp(bl - b)
            st_ref[h] = st * jnp.exp(bl) + lax.dot_general(vb, ke.astype(BF16), _TN,
                                                            preferred_element_type=F32)

            ms = jnp.mean(o * o, axis=-1, keepdims=True)
            rr_ = r_ref[pl.ds(r0, c), vs]
            on = o * lax.rsqrt(ms + EPS) * ng_ref[:, vs] * (rr_ * _sigmoid(rr_))
            ob_ref[pl.ds(r0, c), vs] = on.astype(BF16)
        return carry

    lax.fori_loop(0, tb // c, chunk_body, 0)

    y = jnp.dot(ob_ref[...], wo_ref[...], preferred_element_type=F32)
    o_ref[...] = y * _sigmoid(gm_ref[...])


def _gla_module(proj, a_x, proj_m, a_m, wa2, ba, ng, w_out, nb, ls, d, tb):
    nt = ls // tb
    nqk = d // 2
    c = GLA_CHUNK
    mrow = META_TILE // c - 1

    def cur(col):
        return lambda b, i: (b * nt + i, col)

    const = lambda b, i: (0, 0)
    return pl.pallas_call(
        _gla_kernel,
        grid=(nb, nt),
        in_specs=[
            pl.BlockSpec((tb, nqk), cur(4096 // nqk)),
            pl.BlockSpec((tb, nqk), cur(4096 // nqk + 1)),
            pl.BlockSpec((tb, d), cur(6144 // d)),
            pl.BlockSpec((tb, d), cur(8192 // d)),
            pl.BlockSpec((tb, d), cur(12288 // d)),
            pl.BlockSpec((tb, LANES), cur(0)),
            pl.BlockSpec((c, nqk), lambda b, i: (mrow, 4096 // nqk + 1)),
            pl.BlockSpec((c, d), lambda b, i: (mrow, 6144 // d)),
            pl.BlockSpec((c, LANES), lambda b, i: (mrow, 0)),
            pl.BlockSpec((LANES, nqk), const),
            pl.BlockSpec((1, nqk), const),
            pl.BlockSpec((1, d), const),
            pl.BlockSpec((d, d), const),
        ],
        out_specs=pl.BlockSpec((tb, d), cur(0)),
        out_shape=jax.ShapeDtypeStruct((nb * ls, d), F32),
        scratch_shapes=[
            pltpu.VMEM((GLA_HEADS, d // GLA_HEADS, nqk // GLA_HEADS), F32),
            pltpu.VMEM((tb, d), BF16),
        ],
        compiler_params=_params("parallel", "arbitrary"),
        name="gla_module",
    )(proj, proj, proj, proj, proj, a_x, proj_m, proj_m, a_m, wa2, ba, ng, w_out)


def _mix_kernel(yc_ref, yg_ref, x_ref, wo_ref, g2_ref, wq_ref, h1_ref, xn2_ref, q2_ref):
    m = (yc_ref[...] + yg_ref[...]).astype(BF16)
    h1 = x_ref[...] + jnp.dot(m, wo_ref[...], preferred_element_type=F32)
    h1_ref[...] = h1
    ms = jnp.mean(h1 * h1, axis=-1, keepdims=True)
    xn2 = (h1 * lax.rsqrt(ms + EPS) * g2_ref[...]).astype(BF16)
    xn2_ref[...] = xn2
    q2_ref[...] = jnp.dot(xn2, wq_ref[...], preferred_element_type=F32).astype(BF16)


def _mix_module(yc, yg, x2d, w_out, g2, wq, tm):
    t, d = x2d.shape
    row = lambda i: (i, 0)
    const = lambda i: (0, 0)
    return pl.pallas_call(
        _mix_kernel,
        grid=(t // tm,),
        in_specs=[
            pl.BlockSpec((tm, d), row), pl.BlockSpec((tm, d), row), pl.BlockSpec((tm, d), row),
            pl.BlockSpec((d, d), const), pl.BlockSpec((1, d), const), pl.BlockSpec((d, d), const),
        ],
        out_specs=[pl.BlockSpec((tm, d), row), pl.BlockSpec((tm, d), row), pl.BlockSpec((tm, d), row)],
        out_shape=[jax.ShapeDtypeStruct((t, d), F32), jax.ShapeDtypeStruct((t, d), BF16),
                   jax.ShapeDtypeStruct((t, d), BF16)],
        compiler_params=_params("parallel"),
        name="mix_module",
    )(yc, yg, x2d, w_out, g2, wq)


def _top16(s):
    idx = lax.broadcasted_iota(jnp.int32, s.shape, 0).astype(F32)
    rank = jnp.full(s.shape, 999.0, F32)
    vals = []
    for r in range(PEER_TOPK):
        m = jnp.max(s, axis=0, keepdims=True)
        first = jnp.min(jnp.where(s == m, idx, 1e9), axis=0, keepdims=True)
        sel = idx == first
        rank = jnp.where(sel, float(r), rank)
        s = jnp.where(sel, -jnp.inf, s)
        vals.append(m)
    return rank, jnp.concatenate(vals, axis=0)


def _frontier_counts(v1, v2):
    kk = PEER_TOPK
    r1 = lax.broadcasted_iota(jnp.int32, v1.shape, 0).astype(F32)
    cnt = jnp.zeros(v1.shape, F32)
    front = v1 + v2[0:1, :]
    for it in range(kk):
        m = jnp.max(front, axis=0, keepdims=True)
        first = jnp.min(jnp.where(front == m, r1, 99.0), axis=0, keepdims=True)
        sel = r1 == first
        cnt = cnt + sel.astype(F32)
        if it + 1 < kk:
            nxt = jnp.full(v1.shape, -jnp.inf, F32)
            for r2 in range(1, kk):
                nxt = jnp.where(cnt == float(r2), v2[r2:r2 + 1, :], nxt)
            front = jnp.where(sel, v1 + nxt, front)
    return cnt


def _peer_kernel(xn_ref, q_ref, k1_ref, k2_ref, u_ref, vt_ref, o_ref,
                 yt_ref, s1_ref, s2_ref, p1_ref, c1_ref, r2_ref, p2_ref):
    e = pl.program_id(1)
    tt = xn_ref.shape[0]
    te = u_ref.shape[0]
    nk = PEER_NKEYS
    kk = PEER_TOPK

    @pl.when(e == 0)
    def _():
        yt_ref[...] = jnp.zeros_like(yt_ref)

        def head_body(h, carry):
            c0 = pl.multiple_of(h * 2 * nk, 2 * nk)
            s1_ref[...] = lax.dot_general(k1_ref[h], q_ref[:, pl.ds(c0, nk)], _NT,
                                          preferred_element_type=F32)
            s2_ref[...] = lax.dot_general(k2_ref[h], q_ref[:, pl.ds(c0 + nk, nk)], _NT,
                                          preferred_element_type=F32)

            def lane_body(lt, carry2):
                l0 = pl.multiple_of(lt * LANES, LANES)
                s1 = s1_ref[:, pl.ds(l0, LANES)]
                s2 = s2_ref[:, pl.ds(l0, LANES)]
                rank1, v1 = _top16(s1)
                rank2, v2 = _top16(s2)
                cnt = _frontier_counts(v1, v2)
                p1v = jnp.exp(v1 - v1[0:1, :])
                p2v = jnp.exp(v2 - v2[0:1, :])
                cum = jnp.zeros(v1.shape, F32)
                for r2 in range(kk):
                    cum = cum + jnp.where(cnt > float(r2), p2v[r2:r2 + 1, :], 0.0)
                z = jnp.sum(p1v * cum, axis=0, keepdims=True)
                c1 = jnp.zeros(s1.shape, F32)
                for r in range(kk):
                    c1 = jnp.where(rank1 == float(r), cnt[r:r + 1, :], c1)
                p1_ref[h, :, pl.ds(l0, LANES)] = jnp.exp(s1 - v1[0:1, :]) / z
                c1_ref[h, :, pl.ds(l0, LANES)] = c1
                r2_ref[h, :, pl.ds(l0, LANES)] = rank2
                p2_ref[h, :, pl.ds(l0, LANES)] = jnp.exp(s2 - v2[0:1, :])
                return carry2

            lax.fori_loop(0, tt // LANES, lane_body, 0)
            return carry

        lax.fori_loop(0, PEER_HEADS, head_body, 0)

    ht = lax.dot_general(u_ref[...], xn_ref[...], _NT, preferred_element_type=F32)
    act = 0.5 * ht * (1.0 + lax.erf(ht * (2.0 ** -0.5)))

    pieces = []
    n_i1 = te // nk
    i1_base = pl.multiple_of(e * n_i1, n_i1)
    for ii in range(n_i1):
        cols = []
        for lt in range(tt // LANES):
            ls_ = slice(lt * LANES, (lt + 1) * LANES)
            g = jnp.zeros((nk, LANES), F32)
            for h in range(PEER_HEADS):
                c1 = c1_ref[h, pl.ds(i1_base, n_i1), ls_][ii:ii + 1, :]
                p1 = p1_ref[h, pl.ds(i1_base, n_i1), ls_][ii:ii + 1, :]
                g = g + jnp.where(r2_ref[h, :, ls_] < c1, p2_ref[h, :, ls_] * p1, 0.0)
            cols.append((act[ii * nk:(ii + 1) * nk, ls_] * g).astype(BF16))
        pieces.append(jnp.concatenate(cols, axis=1))
    w = jnp.concatenate(pieces, axis=0)

    yt_ref[...] += jnp.dot(vt_ref[...], w, preferred_element_type=F32)

    @pl.when(e == pl.num_programs(1) - 1)
    def _():
        o_ref[...] = yt_ref[...].T


def _peer_module(xn2, q2, k1, k2, u, vt, tt, te):
    t, d = xn2.shape
    ne = u.shape[0]
    nk = PEER_NKEYS
    sel = pltpu.VMEM((PEER_HEADS, nk, tt), F32)
    return pl.pallas_call(
        _peer_kernel,
        grid=(t // tt, ne // te),
        in_specs=[
            pl.BlockSpec((tt, d), lambda i, e: (i, 0)),
            pl.BlockSpec((tt, d), lambda i, e: (i, 0)),
            pl.BlockSpec((PEER_HEADS, nk, nk), lambda i, e: (0, 0, 0)),
            pl.BlockSpec((PEER_HEADS, nk, nk), lambda i, e: (0, 0, 0)),
            pl.BlockSpec((te, d), lambda i, e: (e, 0)),
            pl.BlockSpec((d, te), lambda i, e: (0, e)),
        ],
        out_specs=pl.BlockSpec((tt, d), lambda i, e: (i, 0)),
        out_shape=jax.ShapeDtypeStruct((t, d), F32),
        scratch_shapes=[
            pltpu.VMEM((d, tt), F32),
            pltpu.VMEM((nk, tt), F32), pltpu.VMEM((nk, tt), F32),
            sel, sel, sel, sel,
        ],
        compiler_params=_params("parallel", "arbitrary"),
        name="peer_module",
    )(xn2, q2, k1, k2, u, vt)


def _final_kernel(h_ref, y_ref, g_ref, o_ref):
    h = h_ref[...] + y_ref[...]
    ms = jnp.mean(h * h, axis=-1, keepdims=True)
    o_ref[...] = h * lax.rsqrt(ms + EPS) * g_ref[...]


def _final_module(h1, y, g, tm):
    t, d = h1.shape
    row = lambda i: (i, 0)
    return pl.pallas_call(
        _final_kernel,
        grid=(t // tm,),
        in_specs=[pl.BlockSpec((tm, d), row), pl.BlockSpec((tm, d), row), pl.BlockSpec((1, d), lambda i: (0, 0))],
        out_specs=pl.BlockSpec((tm, d), row),
        out_shape=jax.ShapeDtypeStruct((t, d), F32),
        compiler_params=_params("parallel"),
        name="final_norm",
    )(h1, y, g)


def _layer(x2d, xm, nb, ls, norm1_g, w_in, w_alpha2, b_alpha, conv_w, conv_b, conv_ln_g, conv_ln_b,
           w_conv_out, gla_norm_g, w_gla_out, w_out, norm2_g, peer_wq, peer_k1, peer_k2, peer_u, peer_v):
    d = x2d.shape[1]
    n_pre = 5 * d
    rank = w_alpha2.shape[0]
    w_main = jnp.concatenate([w_in[:, :n_pre], w_in[:, n_pre + rank:]], axis=1).astype(BF16)
    w_a = jnp.pad(w_in[:, n_pre:n_pre + rank], ((0, 0), (0, LANES - rank))).astype(BF16)
    g1 = norm1_g.reshape(1, d)

    tm = min(1024, x2d.shape[0])
    proj, a_x = _in_proj(x2d, g1, w_main, w_a, tm, 1024)
    proj_m, a_m = _in_proj(xm, g1, w_main, w_a, META_TILE, 1024)

    cw = jnp.pad(conv_w, ((0, CONV_HALO - CONV_WIDTH), (0, 0)))
    yc = _conv_module(proj, proj_m, cw, conv_b.reshape(1, d), conv_ln_g.reshape(1, d), conv_ln_b.reshape(1, d),
                      w_conv_out.astype(BF16), nb, ls, d, 256)

    wa2 = jnp.pad(w_alpha2, ((0, LANES - rank), (0, 0))).astype(BF16)
    yg = _gla_module(proj, a_x, proj_m, a_m, wa2, b_alpha.reshape(1, -1), gla_norm_g.reshape(1, d),
                     w_gla_out.astype(BF16), nb, ls, d, 256)

    h1, xn2, q2 = _mix_module(yc, yg, x2d, w_out.astype(BF16), norm2_g.reshape(1, d), peer_wq.astype(BF16), 256)

    y = _peer_module(xn2, q2, peer_k1.astype(BF16), peer_k2.astype(BF16), peer_u.astype(BF16),
                     peer_v.T.astype(BF16), min(512, x2d.shape[0]), 1024)
    return h1, y


def kernel(x, meta_tokens, norm1_g, w_in, w_alpha2, b_alpha, conv_w, conv_b, conv_ln_g, conv_ln_b, w_conv_out,
           gla_norm_g, w_gla_out, w_out, norm2_g, peer_wq, peer_k1, peer_k2, peer_u, peer_v, normf_g):
    nb, ls, d = x.shape
    depth = w_in.shape[0]
    assert depth == 1, "meta-token positions are only carried through one layer"
    x2d = x.reshape(nb * ls, d)
    xm = jnp.concatenate([jnp.zeros((META_TILE - N_META, d), x.dtype), meta_tokens.astype(x.dtype)], axis=0)
    h1, y = _layer(x2d, xm, nb, ls, norm1_g[0], w_in[0], w_alpha2[0], b_alpha[0], conv_w[0], conv_b[0],
                   conv_ln_g[0], conv_ln_b[0], w_conv_out[0], gla_norm_g[0], w_gla_out[0], w_out[0],
                   norm2_g[0], peer_wq[0], peer_k1[0], peer_k2[0], peer_u[0], peer_v[0])
    out = _final_module(h1, y, normf_g.reshape(1, d), 256)
    return out.reshape(nb, ls, d)
```

```python
import functools

import jax
import jax.numpy as jnp
from jax import lax
from jax.experimental import pallas as pl
from jax.experimental.pallas import tpu as pltpu

F32 = jnp.float32
BF16 = jnp.bfloat16

N_META = 16
CONV_WIDTH = 31
GLA_HEADS = 4
GLA_TAU = 16.0
GLA_CHUNK = 64
PEER_HEADS = 8
PEER_NKEYS = 128
PEER_TOPK = 16
EPS = 1e-6

LANES = 128
META_TILE = 256
CONV_HALO = 32
VMEM_LIMIT_BYTES = 56 * 1024 * 1024

_NT = (((1,), (1,)), ((), ()))
_TN = (((0,), (0,)), ((), ()))


def _sigmoid(x):
    return 1.0 / (1.0 + jnp.exp(-x))


def _params(*sem):
    return pltpu.CompilerParams(dimension_semantics=sem, vmem_limit_bytes=VMEM_LIMIT_BYTES)


def _inproj_kernel(x_ref, g_ref, w_ref, wa_ref, o_ref, a_ref, xn_ref):
    @pl.when(pl.program_id(1) == 0)
    def _():
        rows = 128
        def body(r, _):
            r0 = pl.multiple_of(r * rows, rows)
            x = x_ref[pl.ds(r0, rows), :]
            ms = jnp.mean(x * x, axis=-1, keepdims=True)
            xn_ref[pl.ds(r0, rows), :] = (x * lax.rsqrt(ms + EPS) * g_ref[...]).astype(BF16)
            return 0
        lax.fori_loop(0, x_ref.shape[0] // rows, body, 0)
        a_ref[...] = jnp.dot(xn_ref[...], wa_ref[...], preferred_element_type=F32)

    o_ref[...] = jnp.dot(xn_ref[...], w_ref[...], preferred_element_type=F32)


def _in_proj(x2d, g, w_main, w_a, tm, tn):
    t, d = x2d.shape
    n = w_main.shape[1]
    return pl.pallas_call(
        _inproj_kernel,
        grid=(t // tm, n // tn),
        in_specs=[
            pl.BlockSpec((tm, d), lambda i, j: (i, 0)),
            pl.BlockSpec((1, d), lambda i, j: (0, 0)),
            pl.BlockSpec((d, tn), lambda i, j: (0, j)),
            pl.BlockSpec((d, LANES), lambda i, j: (0, 0)),
        ],
        out_specs=[
            pl.BlockSpec((tm, tn), lambda i, j: (i, j)),
            pl.BlockSpec((tm, LANES), lambda i, j: (i, 0)),
        ],
        out_shape=[jax.ShapeDtypeStruct((t, n), F32), jax.ShapeDtypeStruct((t, LANES), F32)],
        scratch_shapes=[pltpu.VMEM((tm, d), BF16)],
        compiler_params=_params("parallel", "arbitrary"),
        name="in_proj",
    )(x2d, g, w_main, w_a)


def _conv_kernel(a1_ref, a2_ref, p1_ref, p2_ref, m1_ref, m2_ref, cw_ref, cb_ref, lg_ref, lb_ref,
                 wo_ref, gm_ref, o_ref, hbuf_ref, cbuf_ref, hn_ref):
    tc, ch = a1_ref.shape
    first_tile = pl.program_id(1) == 0

    for r0 in range(0, tc, 64):
        hbuf_ref[CONV_HALO + r0:CONV_HALO + r0 + 64, :] = (
            a1_ref[r0:r0 + 64, :] * _sigmoid(a2_ref[r0:r0 + 64, :]))

    @pl.when(first_tile)
    def _():
        hbuf_ref[0:CONV_HALO, :] = m1_ref[...] * _sigmoid(m2_ref[...])

    @pl.when(jnp.logical_not(first_tile))
    def _():
        hbuf_ref[0:CONV_HALO, :] = p1_ref[...] * _sigmoid(p2_ref[...])

    base = CONV_HALO - (CONV_WIDTH - 1)
    for c0 in range(0, ch, LANES):
        acc = jnp.zeros((tc, LANES), F32) + cb_ref[:, c0:c0 + LANES]
        for j in range(CONV_WIDTH):
            acc = acc + hbuf_ref[base + j:base + j + tc, c0:c0 + LANES] * cw_ref[j:j + 1, c0:c0 + LANES]
        cbuf_ref[:, c0:c0 + LANES] = acc

    rows = 32
    def ln_body(r, _):
        r0 = pl.multiple_of(r * rows, rows)
        x = cbuf_ref[pl.ds(r0, rows), :]
        mu = jnp.mean(x, axis=-1, keepdims=True)
        xc = x - mu
        var = jnp.mean(xc * xc, axis=-1, keepdims=True)
        y = xc * lax.rsqrt(var + EPS) * lg_ref[...] + lb_ref[...]
        hn_ref[pl.ds(r0, rows), :] = (y * _sigmoid(y)).astype(BF16)
        return 0
    lax.fori_loop(0, tc // rows, ln_body, 0)

    y = jnp.dot(hn_ref[...], wo_ref[...], preferred_element_type=F32)
    o_ref[...] = y * _sigmoid(gm_ref[...])


def _conv_module(proj, proj_m, conv_w, conv_b, ln_g, ln_b, w_out, nb, ls, ch, tc):
    nt = ls // tc
    hpt = tc // CONV_HALO
    meta_blk = META_TILE // CONV_HALO - 1
    merge_blk = 10240 // ch

    def cur(col):
        return lambda b, i: (b * nt + i, col)

    def prev(col):
        return lambda b, i: (jnp.maximum((b * nt + i) * hpt - 1, 0), col)

    const = lambda b, i: (0, 0)
    return pl.pallas_call(
        _conv_kernel,
        grid=(nb, nt),
        in_specs=[
            pl.BlockSpec((tc, ch), cur(0)),
            pl.BlockSpec((tc, ch), cur(1)),
            pl.BlockSpec((CONV_HALO, ch), prev(0)),
            pl.BlockSpec((CONV_HALO, ch), prev(1)),
            pl.BlockSpec((CONV_HALO, ch), lambda b, i: (meta_blk, 0)),
            pl.BlockSpec((CONV_HALO, ch), lambda b, i: (meta_blk, 1)),
            pl.BlockSpec((CONV_HALO, ch), const),
            pl.BlockSpec((1, ch), const),
            pl.BlockSpec((1, ch), const),
            pl.BlockSpec((1, ch), const),
            pl.BlockSpec((ch, ch), const),
            pl.BlockSpec((tc, ch), cur(merge_blk)),
        ],
        out_specs=pl.BlockSpec((tc, ch), cur(0)),
        out_shape=jax.ShapeDtypeStruct((nb * ls, ch), F32),
        scratch_shapes=[
            pltpu.VMEM((CONV_HALO + tc, ch), F32),
            pltpu.VMEM((tc, ch), F32),
            pltpu.VMEM((tc, ch), BF16),
        ],
        compiler_params=_params("parallel", "arbitrary"),
        name="conv_module",
    )(proj, proj, proj, proj, proj_m, proj_m, conv_w, conv_b, ln_g, ln_b, w_out, proj)


def _gla_kernel(q_ref, k_ref, v_ref, r_ref, gm_ref, a_ref, km_ref, vm_ref, am_ref, wa2_ref, ba_ref,
                ng_ref, wo_ref, o_ref, st_ref, ob_ref):
    nh = GLA_HEADS
    c = GLA_CHUNK
    tb = q_ref.shape[0]
    dk = q_ref.shape[1] // nh
    dv = v_ref.shape[1] // nh
    scale = dk ** -0.5

    row = lax.broadcasted_iota(jnp.int32, (c, c), 0)
    col = lax.broadcasted_iota(jnp.int32, (c, c), 1)
    tri = (col <= row).astype(F32)

    def log_decay(a_c):
        z = jnp.dot(a_c.astype(BF16), wa2_ref[...], preferred_element_type=F32) + ba_ref[...]
        return (jnp.minimum(z, 0.0) - jnp.log1p(jnp.exp(-jnp.abs(z)))) * (1.0 / GLA_TAU)

    def cumsum_rows(g):
        return jnp.dot(tri, g, preferred_element_type=F32, precision=lax.Precision.HIGHEST)

    @pl.when(pl.program_id(1) == 0)
    def _():
        g = log_decay(am_ref[...])
        ridx = lax.broadcasted_iota(jnp.int32, (c, 1), 0)
        g = jnp.where(ridx >= c - N_META, g, 0.0)
        b = cumsum_rows(g)
        bl = b[c - 1:c, :]
        for h in range(nh):
            ks = slice(h * dk, (h + 1) * dk)
            ke = km_ref[:, ks] * jnp.exp(bl[:, ks] - b[:, ks])
            vh = vm_ref[:, h * dv:(h + 1) * dv]
            st_ref[h] = lax.dot_general(vh.astype(BF16), ke.astype(BF16), _TN, preferred_element_type=F32)

    levels = (32, 16, 8)
    level_masks = []
    for s in levels:
        level_masks.append((row // (2 * s) == col // (2 * s)) & ((row // s) % 2 == 1) & ((col // s) % 2 == 0))
    sub = lax.broadcasted_iota(jnp.int32, (8, c), 0)
    lane = lax.broadcasted_iota(jnp.int32, (8, c), 1)

    def chunk_body(ci, carry):
        r0 = pl.multiple_of(ci * c, c)
        g_all = log_decay(a_ref[pl.ds(r0, c), :])
        b_all = cumsum_rows(g_all)
        for h in range(nh):
            ks = slice(h * dk, (h + 1) * dk)
            vs = slice(h * dv, (h + 1) * dv)
            b = b_all[:, ks]
            q = q_ref[pl.ds(r0, c), ks] * scale
            k = k_ref[pl.ds(r0, c), ks]
            vb = v_ref[pl.ds(r0, c), vs].astype(BF16)
            bl = b[c - 1:c, :]
            st = st_ref[h]

            o = lax.dot_general((q * jnp.exp(b)).astype(BF16), st.astype(BF16), _NT,
                                preferred_element_type=F32)

            att = jnp.zeros((c, c), F32)
            for s, msk in zip(levels, level_masks):
                pieces = []
                for p in range(c // (2 * s)):
                    rr = p * 2 * s + s - 1
                    pieces.append(jnp.broadcast_to(b[rr:rr + 1, :], (2 * s, dk)))
                bref = pieces[0] if len(pieces) == 1 else jnp.concatenate(pieces, axis=0)
                qs = q * jnp.exp(jnp.minimum(b - bref, 0.0))
                kk = k * jnp.exp(jnp.minimum(bref - b, 0.0))
                pm = lax.dot_general(qs.astype(BF16), kk.astype(BF16), _NT, preferred_element_type=F32)
                att = att + jnp.where(msk, pm, 0.0)

            blocks = []
            for bi in range(c // 8):
                qb = q[bi * 8:(bi + 1) * 8, :]
                bb = b[bi * 8:(bi + 1) * 8, :]
                blk = jnp.zeros((8, c), F32)
                for jj in range(8):
                    j = bi * 8 + jj
                    e = jnp.exp(jnp.minimum(bb - b[j:j + 1, :], 0.0))
                    colv = jnp.sum(qb * (k[j:j + 1, :] * e), axis=-1, keepdims=True)
                    blk = blk + jnp.where((lane == j) & (sub >= jj), colv, 0.0)
                blocks.append(blk)
            att = att + jnp.concatenate(blocks, axis=0)

            o = o + jnp.dot(att.astype(BF16), vb, preferred_element_type=F32)

            ke = k * jnp.exp(bl - b)
            st_ref[h] = st * jnp.exp(bl) + lax.dot_general(vb, ke.astype(BF16), _TN,
                                                            preferred_element_type=F32)

            ms = jnp.mean(o * o, axis=-1, keepdims=True)
            rr_ = r_ref[pl.ds(r0, c), vs]
            on = o * lax.rsqrt(ms + EPS) * ng_ref[:, vs] * (rr_ * _sigmoid(rr_))
            ob_ref[pl.ds(r0, c), vs] = on.astype(BF16)
        return carry

    lax.fori_loop(0, tb // c, chunk_body, 0)

    y = jnp.dot(ob_ref[...], wo_ref[...], preferred_element_type=F32)
    o_ref[...] = y * _sigmoid(gm_ref[...])


def _gla_module(proj, a_x, proj_m, a_m, wa2, ba, ng, w_out, nb, ls, d, tb):
    nt = ls // tb
    nqk = d // 2
    c = GLA_CHUNK
    mrow = META_TILE // c - 1

    def cur(col):
        return lambda b, i: (b * nt + i, col)

    const = lambda b, i: (0, 0)
    return pl.pallas_call(
        _gla_kernel,
        grid=(nb, nt),
        in_specs=[
            pl.BlockSpec((tb, nqk), cur(4096 // nqk)),
            pl.BlockSpec((tb, nqk), cur(4096 // nqk + 1)),
            pl.BlockSpec((tb, d), cur(6144 // d)),
            pl.BlockSpec((tb, d), cur(8192 // d)),
            pl.BlockSpec((tb, d), cur(12288 // d)),
            pl.BlockSpec((tb, LANES), cur(0)),
            pl.BlockSpec((c, nqk), lambda b, i: (mrow, 4096 // nqk + 1)),
            pl.BlockSpec((c, d), lambda b, i: (mrow, 6144 // d)),
            pl.BlockSpec((c, LANES), lambda b, i: (mrow, 0)),
            pl.BlockSpec((LANES, nqk), const),
            pl.BlockSpec((1, nqk), const),
            pl.BlockSpec((1, d), const),
            pl.BlockSpec((d, d), const),
        ],
        out_specs=pl.BlockSpec((tb, d), cur(0)),
        out_shape=jax.ShapeDtypeStruct((nb * ls, d), F32),
        scratch_shapes=[
            pltpu.VMEM((GLA_HEADS, d // GLA_HEADS, nqk // GLA_HEADS), F32),
            pltpu.VMEM((tb, d), BF16),
        ],
        compiler_params=_params("parallel", "arbitrary"),
        name="gla_module",
    )(proj, proj, proj, proj, proj, a_x, proj_m, proj_m, a_m, wa2, ba, ng, w_out)


def _mix_kernel(yc_ref, yg_ref, x_ref, wo_ref, g2_ref, wq_ref, h1_ref, xn2_ref, q2_ref):
    m = (yc_ref[...] + yg_ref[...]).astype(BF16)
    h1 = x_ref[...] + jnp.dot(m, wo_ref[...], preferred_element_type=F32)
    h1_ref[...] = h1
    ms = jnp.mean(h1 * h1, axis=-1, keepdims=True)
    xn2 = (h1 * lax.rsqrt(ms + EPS) * g2_ref[...]).astype(BF16)
    xn2_ref[...] = xn2
    q2_ref[...] = jnp.dot(xn2, wq_ref[...], preferred_element_type=F32).astype(BF16)


def _mix_module(yc, yg, x2d, w_out, g2, wq, tm):
    t, d = x2d.shape
    row = lambda i: (i, 0)
    const = lambda i: (0, 0)
    return pl.pallas_call(
        _mix_kernel,
        grid=(t // tm,),
        in_specs=[
            pl.BlockSpec((tm, d), row), pl.BlockSpec((tm, d), row), pl.BlockSpec((tm, d), row),
            pl.BlockSpec((d, d), const), pl.BlockSpec((1, d), const), pl.BlockSpec((d, d), const),
        ],
        out_specs=[pl.BlockSpec((tm, d), row), pl.BlockSpec((tm, d), row), pl.BlockSpec((tm, d), row)],
        out_shape=[jax.ShapeDtypeStruct((t, d), F32), jax.ShapeDtypeStruct((t, d), BF16),
                   jax.ShapeDtypeStruct((t, d), BF16)],
        compiler_params=_params("parallel"),
        name="mix_module",
    )(yc, yg, x2d, w_out, g2, wq)


_NO_RANK = 100.0


def _top16(s):
    idx = lax.broadcasted_iota(jnp.int32, s.shape, 0).astype(F32)
    rank = jnp.full(s.shape, _NO_RANK, F32)
    vals = []
    for r in range(PEER_TOPK):
        m = jnp.max(s, axis=0, keepdims=True)
        first = jnp.min(jnp.where(s == m, idx, 1e9), axis=0, keepdims=True)
        sel = idx == first
        rank = jnp.where(sel, float(r), rank)
        s = jnp.where(sel, -jnp.inf, s)
        vals.append(m)
    return rank, jnp.concatenate(vals, axis=0)


def _frontier_counts(v1, v2):
    kk = PEER_TOPK
    r1 = lax.broadcasted_iota(jnp.int32, v1.shape, 0).astype(F32)
    cnt = jnp.zeros(v1.shape, F32)
    front = v1 + v2[0:1, :]
    for it in range(kk):
        m = jnp.max(front, axis=0, keepdims=True)
        first = jnp.min(jnp.where(front == m, r1, 99.0), axis=0, keepdims=True)
        sel = r1 == first
        cnt = cnt + sel.astype(F32)
        if it + 1 < kk:
            nxt = jnp.full(v1.shape, -jnp.inf, F32)
            for r2 in range(1, kk):
                nxt = jnp.where(cnt == float(r2), v2[r2:r2 + 1, :], nxt)
            front = jnp.where(sel, v1 + nxt, front)
    return cnt


def _peer_kernel(xn_ref, q_ref, k1_ref, k2_ref, u_ref, vt_ref, o_ref,
                 yt_ref, hta_ref, htb_ref, w_ref, s1_ref, s2_ref, p1_ref, c1_ref, r2_ref, p2_ref):
    e = pl.program_id(1)
    n_steps = pl.num_programs(1)
    tt = xn_ref.shape[0]
    te = u_ref.shape[0]
    nk = PEER_NKEYS
    kk = PEER_TOPK
    n_i1 = te // nk
    pack = 16

    def expert_acts():
        return lax.dot_general(u_ref[...], xn_ref[...], _NT, preferred_element_type=F32)

    @pl.when(e == 0)
    def _():
        yt_ref[...] = jnp.zeros_like(yt_ref)
        hta_ref[...] = expert_acts()

        def head_body(h, carry):
            c0 = pl.multiple_of(h * 2 * nk, 2 * nk)
            s1_ref[...] = lax.dot_general(k1_ref[h], q_ref[:, pl.ds(c0, nk)], _NT,
                                          preferred_element_type=F32)
            s2_ref[...] = lax.dot_general(k2_ref[h], q_ref[:, pl.ds(c0 + nk, nk)], _NT,
                                          preferred_element_type=F32)

            def lane_body(lt, carry2):
                l0 = pl.multiple_of(lt * LANES, LANES)
                s1 = s1_ref[:, pl.ds(l0, LANES)]
                s2 = s2_ref[:, pl.ds(l0, LANES)]
                rank1, v1 = _top16(s1)
                rank2, v2 = _top16(s2)
                cnt = _frontier_counts(v1, v2)
                p1v = jnp.exp(v1 - v1[0:1, :])
                p2v = jnp.exp(v2 - v2[0:1, :])
                cum = jnp.zeros(v1.shape, F32)
                for r2 in range(kk):
                    cum = cum + jnp.where(cnt > float(r2), p2v[r2:r2 + 1, :], 0.0)
                z = jnp.sum(p1v * cum, axis=0, keepdims=True)
                c1 = jnp.zeros(s1.shape, F32)
                for r in range(kk):
                    c1 = jnp.where(rank1 == float(r), cnt[r:r + 1, :], c1)
                p1_ref[h, :, pl.ds(l0, LANES)] = jnp.exp(s1 - v1[0:1, :]) / z
                c1_ref[h, :, pl.ds(l0, LANES)] = c1
                r2_ref[h, :, pl.ds(l0, LANES)] = rank2.astype(BF16)
                p2_ref[h, :, pl.ds(l0, LANES)] = jnp.exp(s2 - v2[0:1, :]).astype(BF16)
                return carry2

            lax.fori_loop(0, tt // LANES, lane_body, 0)
            return carry

        lax.fori_loop(0, PEER_HEADS, head_body, 0)

    def main_step(ht_w, ht_r):
        ht_w[...] = expert_acts()
        i1_base = pl.multiple_of((e - 1) * n_i1, n_i1)
        for lt in range(tt // LANES):
            ls_ = slice(lt * LANES, (lt + 1) * LANES)
            c1s = [c1_ref[h, pl.ds(i1_base, n_i1), ls_] for h in range(PEER_HEADS)]
            p1s = [p1_ref[h, pl.ds(i1_base, n_i1), ls_] for h in range(PEER_HEADS)]
            for ii in range(n_i1):
                rows = slice(ii * nk, (ii + 1) * nk)
                c1b = [jnp.broadcast_to(c[ii:ii + 1, :], (pack, LANES)).astype(BF16) for c in c1s]
                p1b = [jnp.broadcast_to(p[ii:ii + 1, :], (pack, LANES)).astype(BF16) for p in p1s]
                x = ht_r[rows, ls_]
                act = (0.5 * x * (1.0 + lax.erf(x * (2.0 ** -0.5)))).astype(BF16)
                for rg in range(nk // pack):
                    rs = slice(rg * pack, (rg + 1) * pack)
                    g = jnp.zeros((pack, LANES), BF16)
                    for h in range(PEER_HEADS):
                        g = g + jnp.where(r2_ref[h, rs, ls_] < c1b[h], p2_ref[h, rs, ls_] * p1b[h],
                                          jnp.zeros((), BF16))
                    w_ref[ii * nk + rg * pack:ii * nk + (rg + 1) * pack, ls_] = act[rs, :] * g
        yt_ref[...] += jnp.dot(vt_ref[...], w_ref[...], preferred_element_type=F32)

    odd = lax.rem(e, 2) == 1

    @pl.when(odd)
    def _():
        main_step(htb_ref, hta_ref)

    @pl.when(jnp.logical_and(e > 0, jnp.logical_not(odd)))
    def _():
        main_step(hta_ref, htb_ref)

    @pl.when(e == n_steps - 1)
    def _():
        o_ref[...] = yt_ref[...].T


def _peer_module(xn2, q2, k1, k2, u, vt, tt, te):
    t, d = xn2.shape
    ne = u.shape[0] // te
    nk = PEER_NKEYS
    sel32 = pltpu.VMEM((PEER_HEADS, nk, tt), F32)
    sel16 = pltpu.VMEM((PEER_HEADS, nk, tt), BF16)
    return pl.pallas_call(
        _peer_kernel,
        grid=(t // tt, ne + 1),
        in_specs=[
            pl.BlockSpec((tt, d), lambda i, e: (i, 0)),
            pl.BlockSpec((tt, d), lambda i, e: (i, 0)),
            pl.BlockSpec((PEER_HEADS, nk, nk), lambda i, e: (0, 0, 0)),
            pl.BlockSpec((PEER_HEADS, nk, nk), lambda i, e: (0, 0, 0)),
            pl.BlockSpec((te, d), lambda i, e: (jnp.minimum(e, ne - 1), 0)),
            pl.BlockSpec((d, te), lambda i, e: (0, jnp.maximum(e - 1, 0))),
        ],
        out_specs=pl.BlockSpec((tt, d), lambda i, e: (i, 0)),
        out_shape=jax.ShapeDtypeStruct((t, d), F32),
        scratch_shapes=[
            pltpu.VMEM((d, tt), F32),
            pltpu.VMEM((te, tt), F32), pltpu.VMEM((te, tt), F32),
            pltpu.VMEM((te, tt), BF16),
            pltpu.VMEM((nk, tt), F32), pltpu.VMEM((nk, tt), F32),
            sel32, sel32, sel16, sel16,
        ],
        compiler_params=_params("parallel", "arbitrary"),
        name="peer_module",
    )(xn2, q2, k1, k2, u, vt)


def _final_kernel(h_ref, y_ref, g_ref, o_ref):
    h = h_ref[...] + y_ref[...]
    ms = jnp.mean(h * h, axis=-1, keepdims=True)
    o_ref[...] = h * lax.rsqrt(ms + EPS) * g_ref[...]


def _final_module(h1, y, g, tm):
    t, d = h1.shape
    row = lambda i: (i, 0)
    return pl.pallas_call(
        _final_kernel,
        grid=(t // tm,),
        in_specs=[pl.BlockSpec((tm, d), row), pl.BlockSpec((tm, d), row), pl.BlockSpec((1, d), lambda i: (0, 0))],
        out_specs=pl.BlockSpec((tm, d), row),
        out_shape=jax.ShapeDtypeStruct((t, d), F32),
        compiler_params=_params("parallel"),
        name="final_norm",
    )(h1, y, g)


def _layer(x2d, xm, nb, ls, norm1_g, w_in, w_alpha2, b_alpha, conv_w, conv_b, conv_ln_g, conv_ln_b,
           w_conv_out, gla_norm_g, w_gla_out, w_out, norm2_g, peer_wq, peer_k1, peer_k2, peer_u, peer_v):
    d = x2d.shape[1]
    n_pre = 5 * d
    rank = w_alpha2.shape[0]
    w_main = jnp.concatenate([w_in[:, :n_pre], w_in[:, n_pre + rank:]], axis=1).astype(BF16)
    w_a = jnp.pad(w_in[:, n_pre:n_pre + rank], ((0, 0), (0, LANES - rank))).astype(BF16)
    g1 = norm1_g.reshape(1, d)

    tm = min(1024, x2d.shape[0])
    proj, a_x = _in_proj(x2d, g1, w_main, w_a, tm, 1024)
    proj_m, a_m = _in_proj(xm, g1, w_main, w_a, META_TILE, 1024)

    cw = jnp.pad(conv_w, ((0, CONV_HALO - CONV_WIDTH), (0, 0)))
    yc = _conv_module(proj, proj_m, cw, conv_b.reshape(1, d), conv_ln_g.reshape(1, d), conv_ln_b.reshape(1, d),
                      w_conv_out.astype(BF16), nb, ls, d, 256)

    wa2 = jnp.pad(w_alpha2, ((0, LANES - rank), (0, 0))).astype(BF16)
    yg = _gla_module(proj, a_x, proj_m, a_m, wa2, b_alpha.reshape(1, -1), gla_norm_g.reshape(1, d),
                     w_gla_out.astype(BF16), nb, ls, d, 256)

    h1, xn2, q2 = _mix_module(yc, yg, x2d, w_out.astype(BF16), norm2_g.reshape(1, d), peer_wq.astype(BF16), 256)

    y = _peer_module(xn2, q2, peer_k1.astype(BF16), peer_k2.astype(BF16), peer_u.astype(BF16),
                     peer_v.T.astype(BF16), min(512, x2d.shape[0]), 1024)
    return h1, y


def kernel(x, meta_tokens, norm1_g, w_in, w_alpha2, b_alpha, conv_w, conv_b, conv_ln_g, conv_ln_b, w_conv_out,
           gla_norm_g, w_gla_out, w_out, norm2_g, peer_wq, peer_k1, peer_k2, peer_u, peer_v, normf_g):
    nb, ls, d = x.shape
    depth = w_in.shape[0]
    assert depth == 1, "meta-token positions are only carried through one layer"
    x2d = x.reshape(nb * ls, d)
    xm = jnp.concatenate([jnp.zeros((META_TILE - N_META, d), x.dtype), meta_tokens.astype(x.dtype)], axis=0)
    h1, y = _layer(x2d, xm, nb, ls, norm1_g[0], w_in[0], w_alpha2[0], b_alpha[0], conv_w[0], conv_b[0],
                   conv_ln_g[0], conv_ln_b[0], w_conv_out[0], gla_norm_g[0], w_gla_out[0], w_out[0],
                   norm2_g[0], peer_wq[0], peer_k1[0], peer_k2[0], peer_u[0], peer_v[0])
    out = _final_module(h1, y, normf_g.reshape(1, d), 256)
    return out.reshape(nb, ls, d)
```

```python
import functools

import jax
import jax.numpy as jnp
from jax import lax
from jax.experimental import pallas as pl
from jax.experimental.pallas import tpu as pltpu

F32 = jnp.float32
BF16 = jnp.bfloat16

N_META = 16
CONV_WIDTH = 31
GLA_HEADS = 4
GLA_TAU = 16.0
GLA_CHUNK = 64
PEER_HEADS = 8
PEER_NKEYS = 128
PEER_TOPK = 16
EPS = 1e-6

LANES = 128
META_TILE = 256
CONV_HALO = 32
INPROJ_TN = 1024
PEER_TE = 1024
VMEM_LIMIT_BYTES = 56 * 1024 * 1024

_NT = (((1,), (1,)), ((), ()))
_TN = (((0,), (0,)), ((), ()))


def _sigmoid(x):
    return 1.0 / (1.0 + jnp.exp(-x))


def _params(*sem):
    return pltpu.CompilerParams(dimension_semantics=sem, vmem_limit_bytes=VMEM_LIMIT_BYTES)


def _inproj_kernel(x_ref, g_ref, w_ref, wa_ref, o_ref, a_ref, xn_ref):
    @pl.when(pl.program_id(1) == 0)
    def _():
        rows = 128
        def body(r, _):
            r0 = pl.multiple_of(r * rows, rows)
            x = x_ref[pl.ds(r0, rows), :]
            ms = jnp.mean(x * x, axis=-1, keepdims=True)
            xn_ref[pl.ds(r0, rows), :] = (x * lax.rsqrt(ms + EPS) * g_ref[...]).astype(BF16)
            return 0
        lax.fori_loop(0, x_ref.shape[0] // rows, body, 0)
        a_ref[...] = jnp.dot(xn_ref[...], wa_ref[...], preferred_element_type=F32)

    o_ref[...] = jnp.dot(xn_ref[...], w_ref[...], preferred_element_type=F32)


def _in_proj(x2d, g, w_tiles, w_a, tm):
    t, d = x2d.shape
    n_tiles, _, tn = w_tiles.shape
    n = n_tiles * tn
    return pl.pallas_call(
        _inproj_kernel,
        grid=(t // tm, n_tiles),
        in_specs=[
            pl.BlockSpec((tm, d), lambda i, j: (i, 0)),
            pl.BlockSpec((1, d), lambda i, j: (0, 0)),
            pl.BlockSpec((None, d, tn), lambda i, j: (j, 0, 0)),
            pl.BlockSpec((d, LANES), lambda i, j: (0, 0)),
        ],
        out_specs=[
            pl.BlockSpec((tm, tn), lambda i, j: (i, j)),
            pl.BlockSpec((tm, LANES), lambda i, j: (i, 0)),
        ],
        out_shape=[jax.ShapeDtypeStruct((t, n), F32), jax.ShapeDtypeStruct((t, LANES), F32)],
        scratch_shapes=[pltpu.VMEM((tm, d), BF16)],
        compiler_params=_params("parallel", "arbitrary"),
        name="in_proj",
    )(x2d, g, w_tiles, w_a)


def _conv_kernel(a1_ref, a2_ref, p1_ref, p2_ref, m1_ref, m2_ref, cw_ref, cb_ref, lg_ref, lb_ref,
                 wo_ref, gm_ref, o_ref, hbuf_ref, cbuf_ref, hn_ref):
    tc, ch = a1_ref.shape
    first_tile = pl.program_id(1) == 0

    for r0 in range(0, tc, 64):
        hbuf_ref[CONV_HALO + r0:CONV_HALO + r0 + 64, :] = (
            a1_ref[r0:r0 + 64, :] * _sigmoid(a2_ref[r0:r0 + 64, :]))

    @pl.when(first_tile)
    def _():
        hbuf_ref[0:CONV_HALO, :] = m1_ref[...] * _sigmoid(m2_ref[...])

    @pl.when(jnp.logical_not(first_tile))
    def _():
        hbuf_ref[0:CONV_HALO, :] = p1_ref[...] * _sigmoid(p2_ref[...])

    sub = 8
    rb = 128
    for c0 in range(0, ch, LANES):
        lanes = slice(c0, c0 + LANES)
        for t0 in range(0, tc, rb):
            acc = jnp.zeros((rb, LANES), F32) + cb_ref[:, lanes]
            for s in range(sub):
                lead = 0 if s == 0 else sub
                part = None
                for a in range((CONV_WIDTH - 1 - s) // sub + 1):
                    j = CONV_WIDTH - 1 - (sub * a + s)
                    r0 = CONV_HALO + t0 - lead - sub * a
                    term = hbuf_ref[r0:r0 + rb + lead, lanes] * cw_ref[j:j + 1, lanes]
                    part = term if part is None else part + term
                acc = acc + part[lead - s:lead - s + rb, :]
            cbuf_ref[t0:t0 + rb, lanes] = acc

    rows = 32
    def ln_body(r, _):
        r0 = pl.multiple_of(r * rows, rows)
        x = cbuf_ref[pl.ds(r0, rows), :]
        mu = jnp.mean(x, axis=-1, keepdims=True)
        xc = x - mu
        var = jnp.mean(xc * xc, axis=-1, keepdims=True)
        y = xc * lax.rsqrt(var + EPS) * lg_ref[...] + lb_ref[...]
        hn_ref[pl.ds(r0, rows), :] = (y * _sigmoid(y)).astype(BF16)
        return 0
    lax.fori_loop(0, tc // rows, ln_body, 0)

    y = jnp.dot(hn_ref[...], wo_ref[...], preferred_element_type=F32)
    o_ref[...] = y * _sigmoid(gm_ref[...])


def _conv_module(proj, proj_m, conv_w, conv_b, ln_g, ln_b, w_out, nb, ls, ch, tc):
    nt = ls // tc
    hpt = tc // CONV_HALO
    meta_blk = META_TILE // CONV_HALO - 1
    merge_blk = 10240 // ch

    def cur(col):
        return lambda b, i: (b * nt + i, col)

    def prev(col):
        return lambda b, i: (jnp.maximum((b * nt + i) * hpt - 1, 0), col)

    const = lambda b, i: (0, 0)
    return pl.pallas_call(
        _conv_kernel,
        grid=(nb, nt),
        in_specs=[
            pl.BlockSpec((tc, ch), cur(0)),
            pl.BlockSpec((tc, ch), cur(1)),
            pl.BlockSpec((CONV_HALO, ch), prev(0)),
            pl.BlockSpec((CONV_HALO, ch), prev(1)),
            pl.BlockSpec((CONV_HALO, ch), lambda b, i: (meta_blk, 0)),
            pl.BlockSpec((CONV_HALO, ch), lambda b, i: (meta_blk, 1)),
            pl.BlockSpec((CONV_HALO, ch), const),
            pl.BlockSpec((1, ch), const),
            pl.BlockSpec((1, ch), const),
            pl.BlockSpec((1, ch), const),
            pl.BlockSpec((ch, ch), const),
            pl.BlockSpec((tc, ch), cur(merge_blk)),
        ],
        out_specs=pl.BlockSpec((tc, ch), cur(0)),
        out_shape=jax.ShapeDtypeStruct((nb * ls, ch), F32),
        scratch_shapes=[
            pltpu.VMEM((CONV_HALO + tc, ch), F32),
            pltpu.VMEM((tc, ch), F32),
            pltpu.VMEM((tc, ch), BF16),
        ],
        compiler_params=_params("parallel", "arbitrary"),
        name="conv_module",
    )(proj, proj, proj, proj, proj_m, proj_m, conv_w, conv_b, ln_g, ln_b, w_out, proj)


def _gla_kernel(q_ref, k_ref, v_ref, r_ref, gm_ref, a_ref, km_ref, vm_ref, am_ref, wa2_ref, ba_ref,
                ng_ref, wo_ref, o_ref, st_ref, ob_ref):
    nh = GLA_HEADS
    c = GLA_CHUNK
    tb = q_ref.shape[0]
    dk = q_ref.shape[1] // nh
    dv = v_ref.shape[1] // nh
    scale = dk ** -0.5

    row = lax.broadcasted_iota(jnp.int32, (c, c), 0)
    col = lax.broadcasted_iota(jnp.int32, (c, c), 1)
    tri = (col <= row).astype(F32)

    def log_decay(a_c):
        z = jnp.dot(a_c.astype(BF16), wa2_ref[...], preferred_element_type=F32) + ba_ref[...]
        return (jnp.minimum(z, 0.0) - jnp.log1p(jnp.exp(-jnp.abs(z)))) * (1.0 / GLA_TAU)

    def cumsum_rows(g):
        return jnp.dot(tri, g, preferred_element_type=F32, precision=lax.Precision.HIGHEST)

    @pl.when(pl.program_id(1) == 0)
    def _():
        g = log_decay(am_ref[...])
        ridx = lax.broadcasted_iota(jnp.int32, (c, 1), 0)
        g = jnp.where(ridx >= c - N_META, g, 0.0)
        b = cumsum_rows(g)
        bl = b[c - 1:c, :]
        for h in range(nh):
            ks = slice(h * dk, (h + 1) * dk)
            ke = km_ref[:, ks] * jnp.exp(bl[:, ks] - b[:, ks])
            vh = vm_ref[:, h * dv:(h + 1) * dv]
            st_ref[h] = lax.dot_general(vh.astype(BF16), ke.astype(BF16), _TN, preferred_element_type=F32)

    levels = (32, 16, 8)
    level_masks = []
    for s in levels:
        level_masks.append((row // (2 * s) == col // (2 * s)) & ((row // s) % 2 == 1) & ((col // s) % 2 == 0))
    sub = lax.broadcasted_iota(jnp.int32, (8, c), 0)
    lane = lax.broadcasted_iota(jnp.int32, (8, c), 1)

    def chunk_body(ci, carry):
        r0 = pl.multiple_of(ci * c, c)
        g_all = log_decay(a_ref[pl.ds(r0, c), :])
        b_all = cumsum_rows(g_all)
        for h in range(nh):
            ks = slice(h * dk, (h + 1) * dk)
            vs = slice(h * dv, (h + 1) * dv)
            b = b_all[:, ks]
            q = q_ref[pl.ds(r0, c), ks] * scale
            k = k_ref[pl.ds(r0, c), ks]
            vb = v_ref[pl.ds(r0, c), vs].astype(BF16)
            bl = b[c - 1:c, :]
            st = st_ref[h]

            o = lax.dot_general((q * jnp.exp(b)).astype(BF16), st.astype(BF16), _NT,
                                preferred_element_type=F32)

            att = jnp.zeros((c, c), F32)
            for s, msk in zip(levels, level_masks):
                pieces = []
                for p in range(c // (2 * s)):
                    rr = p * 2 * s + s - 1
                    pieces.append(jnp.broadcast_to(b[rr:rr + 1, :], (2 * s, dk)))
                bref = pieces[0] if len(pieces) == 1 else jnp.concatenate(pieces, axis=0)
                qs = q * jnp.exp(jnp.minimum(b - bref, 0.0))
                kk = k * jnp.exp(jnp.minimum(bref - b, 0.0))
                pm = lax.dot_general(qs.astype(BF16), kk.astype(BF16), _NT, preferred_element_type=F32)
                att = att + jnp.where(msk, pm, 0.0)

            blocks = []
            for bi in range(c // 8):
                qb = q[bi * 8:(bi + 1) * 8, :]
                bb = b[bi * 8:(bi + 1) * 8, :]
                blk = jnp.zeros((8, c), F32)
                for jj in range(8):
                    j = bi * 8 + jj
                    e = jnp.exp(jnp.minimum(bb - b[j:j + 1, :], 0.0))
                    colv = jnp.sum(qb * (k[j:j + 1, :] * e), axis=-1, keepdims=True)
                    blk = blk + jnp.where((lane == j) & (sub >= jj), colv, 0.0)
                blocks.append(blk)
            att = att + jnp.concatenate(blocks, axis=0)

            o = o + jnp.dot(att.astype(BF16), vb, preferred_element_type=F32)

            ke = k * jnp.exp(bl - b)
            st_ref[h] = st * jnp.exp(bl) + lax.dot_general(vb, ke.astype(BF16), _TN,
                                                            preferred_element_type=F32)

            ms = jnp.mean(o * o, axis=-1, keepdims=True)
            rr_ = r_ref[pl.ds(r0, c), vs]
            on = o * lax.rsqrt(ms + EPS) * ng_ref[:, vs] * (rr_ * _sigmoid(rr_))
            ob_ref[pl.ds(r0, c), vs] = on.astype(BF16)
        return carry

    lax.fori_loop(0, tb // c, chunk_body, 0)

    y = jnp.dot(ob_ref[...], wo_ref[...], preferred_element_type=F32)
    o_ref[...] = y * _sigmoid(gm_ref[...])


def _gla_module(proj, a_x, proj_m, a_m, wa2, ba, ng, w_out, nb, ls, d, tb):
    nt = ls // tb
    nqk = d // 2
    c = GLA_CHUNK
    mrow = META_TILE // c - 1

    def cur(col):
        return lambda b, i: (b * nt + i, col)

    const = lambda b, i: (0, 0)
    return pl.pallas_call(
        _gla_kernel,
        grid=(nb, nt),
        in_specs=[
            pl.BlockSpec((tb, nqk), cur(4096 // nqk)),
            pl.BlockSpec((tb, nqk), cur(4096 // nqk + 1)),
            pl.BlockSpec((tb, d), cur(6144 // d)),
            pl.BlockSpec((tb, d), cur(8192 // d)),
            pl.BlockSpec((tb, d), cur(12288 // d)),
            pl.BlockSpec((tb, LANES), cur(0)),
            pl.BlockSpec((c, nqk), lambda b, i: (mrow, 4096 // nqk + 1)),
            pl.BlockSpec((c, d), lambda b, i: (mrow, 6144 // d)),
            pl.BlockSpec((c, LANES), lambda b, i: (mrow, 0)),
            pl.BlockSpec((LANES, nqk), const),
            pl.BlockSpec((1, nqk), const),
            pl.BlockSpec((1, d), const),
            pl.BlockSpec((d, d), const),
        ],
        out_specs=pl.BlockSpec((tb, d), cur(0)),
        out_shape=jax.ShapeDtypeStruct((nb * ls, d), F32),
        scratch_shapes=[
            pltpu.VMEM((GLA_HEADS, d // GLA_HEADS, nqk // GLA_HEADS), F32),
            pltpu.VMEM((tb, d), BF16),
        ],
        compiler_params=_params("parallel", "arbitrary"),
        name="gla_module",
    )(proj, proj, proj, proj, proj, a_x, proj_m, proj_m, a_m, wa2, ba, ng, w_out)


def _mix_kernel(yc_ref, yg_ref, x_ref, wo_ref, g2_ref, wq_ref, h1_ref, xn2_ref, q2_ref):
    m = (yc_ref[...] + yg_ref[...]).astype(BF16)
    h1 = x_ref[...] + jnp.dot(m, wo_ref[...], preferred_element_type=F32)
    h1_ref[...] = h1
    ms = jnp.mean(h1 * h1, axis=-1, keepdims=True)
    xn2 = (h1 * lax.rsqrt(ms + EPS) * g2_ref[...]).astype(BF16)
    xn2_ref[...] = xn2
    q2_ref[...] = jnp.dot(xn2, wq_ref[...], preferred_element_type=F32).astype(BF16)


def _mix_module(yc, yg, x2d, w_out, g2, wq, tm):
    t, d = x2d.shape
    row = lambda i: (i, 0)
    const = lambda i: (0, 0)
    return pl.pallas_call(
        _mix_kernel,
        grid=(t // tm,),
        in_specs=[
            pl.BlockSpec((tm, d), row), pl.BlockSpec((tm, d), row), pl.BlockSpec((tm, d), row),
            pl.BlockSpec((d, d), const), pl.BlockSpec((1, d), const), pl.BlockSpec((d, d), const),
        ],
        out_specs=[pl.BlockSpec((tm, d), row), pl.BlockSpec((tm, d), row), pl.BlockSpec((tm, d), row)],
        out_shape=[jax.ShapeDtypeStruct((t, d), F32), jax.ShapeDtypeStruct((t, d), BF16),
                   jax.ShapeDtypeStruct((t, d), BF16)],
        compiler_params=_params("parallel"),
        name="mix_module",
    )(yc, yg, x2d, w_out, g2, wq)


_NO_RANK = 100.0


def _top16(s):
    idx = lax.broadcasted_iota(jnp.int32, s.shape, 0).astype(F32)
    rank = jnp.full(s.shape, _NO_RANK, F32)
    vals = []
    for r in range(PEER_TOPK):
        m = jnp.max(s, axis=0, keepdims=True)
        first = jnp.min(jnp.where(s == m, idx, 1e9), axis=0, keepdims=True)
        sel = idx == first
        rank = jnp.where(sel, float(r), rank)
        s = jnp.where(sel, -jnp.inf, s)
        vals.append(m)
    return rank, jnp.concatenate(vals, axis=0)


def _top16_distinct(s):
    rank = jnp.full(s.shape, _NO_RANK, F32)
    vals = []
    for r in range(PEER_TOPK):
        m = jnp.max(s, axis=0, keepdims=True)
        sel = s == m
        rank = jnp.where(sel, float(r), rank)
        s = jnp.where(sel, -jnp.inf, s)
        vals.append(m)
    n_ranked = jnp.sum(jnp.where(rank < float(PEER_TOPK), 1.0, 0.0), axis=0, keepdims=True)
    return rank, jnp.concatenate(vals, axis=0), n_ranked


def _frontier_counts(v1, v2):
    kk = PEER_TOPK
    r1 = lax.broadcasted_iota(jnp.int32, v1.shape, 0).astype(F32)
    cnt = jnp.zeros(v1.shape, F32)
    front = v1 + v2[0:1, :]
    for it in range(kk):
        m = jnp.max(front, axis=0, keepdims=True)
        first = jnp.min(jnp.where(front == m, r1, 99.0), axis=0, keepdims=True)
        sel = r1 == first
        cnt = cnt + sel.astype(F32)
        if it + 1 < kk:
            nxt = jnp.full(v1.shape, -jnp.inf, F32)
            for r2 in range(1, kk):
                nxt = jnp.where(cnt == float(r2), v2[r2:r2 + 1, :], nxt)
            front = jnp.where(sel, v1 + nxt, front)
    return cnt


def _peer_kernel(xn_ref, q_ref, k1_ref, k2_ref, u_ref, vt_ref, o_ref,
                 yt_ref, hta_ref, htb_ref, w_ref, s1_ref, s2_ref, p1_ref, c1_ref, r2_ref, p2_ref):
    e = pl.program_id(1)
    n_steps = pl.num_programs(1)
    tt = xn_ref.shape[0]
    te = u_ref.shape[0]
    nk = PEER_NKEYS
    kk = PEER_TOPK
    n_i1 = te // nk
    pack = 16

    def expert_acts():
        return lax.dot_general(u_ref[...], xn_ref[...], _NT, preferred_element_type=F32)

    @pl.when(e == 0)
    def _():
        yt_ref[...] = jnp.zeros_like(yt_ref)
        hta_ref[...] = expert_acts()

        def head_body(h, carry):
            c0 = pl.multiple_of(h * 2 * nk, 2 * nk)
            s1_ref[...] = lax.dot_general(k1_ref[h], q_ref[:, pl.ds(c0, nk)], _NT,
                                          preferred_element_type=F32)
            s2_ref[...] = lax.dot_general(k2_ref[h], q_ref[:, pl.ds(c0 + nk, nk)], _NT,
                                          preferred_element_type=F32)

            def lane_body(lt, carry2):
                l0 = pl.multiple_of(lt * LANES, LANES)
                s1 = s1_ref[:, pl.ds(l0, LANES)]
                s2 = s2_ref[:, pl.ds(l0, LANES)]

                def route(rank1, v1, rank2, v2):
                    cnt = _frontier_counts(v1, v2)
                    p1v = jnp.exp(v1 - v1[0:1, :])
                    p2v = jnp.exp(v2 - v2[0:1, :])
                    cum = jnp.zeros(v1.shape, F32)
                    for r2 in range(kk):
                        cum = cum + jnp.where(cnt > float(r2), p2v[r2:r2 + 1, :], 0.0)
                    z = jnp.sum(p1v * cum, axis=0, keepdims=True)
                    c1 = jnp.zeros(s1.shape, F32)
                    for r in range(kk):
                        c1 = jnp.where(rank1 == float(r), cnt[r:r + 1, :], c1)
                    p1_ref[h, :, pl.ds(l0, LANES)] = jnp.exp(s1 - v1[0:1, :]) / z
                    c1_ref[h, :, pl.ds(l0, LANES)] = c1
                    r2_ref[h, :, pl.ds(l0, LANES)] = rank2.astype(BF16)
                    p2_ref[h, :, pl.ds(l0, LANES)] = jnp.exp(s2 - v2[0:1, :]).astype(BF16)

                rank1, v1, n1 = _top16_distinct(s1)
                rank2, v2, n2 = _top16_distinct(s2)
                route(rank1, v1, rank2, v2)
                ties = jnp.max(jnp.abs(n1 - float(kk)) + jnp.abs(n2 - float(kk)))

                @pl.when(ties > 0.0)
                def _():
                    route(*_top16(s1), *_top16(s2))

                return carry2

            lax.fori_loop(0, tt // LANES, lane_body, 0)
            return carry

        lax.fori_loop(0, PEER_HEADS, head_body, 0)

    def main_step(ht_w, ht_r):
        ht_w[...] = expert_acts()
        i1_base = pl.multiple_of((e - 1) * n_i1, n_i1)
        for lt in range(tt // LANES):
            ls_ = slice(lt * LANES, (lt + 1) * LANES)
            c1s = [c1_ref[h, pl.ds(i1_base, n_i1), ls_] for h in range(PEER_HEADS)]
            p1s = [p1_ref[h, pl.ds(i1_base, n_i1), ls_] for h in range(PEER_HEADS)]
            for ii in range(n_i1):
                rows = slice(ii * nk, (ii + 1) * nk)
                c1b = [jnp.broadcast_to(c[ii:ii + 1, :], (pack, LANES)).astype(BF16) for c in c1s]
                p1b = [jnp.broadcast_to(p[ii:ii + 1, :], (pack, LANES)).astype(BF16) for p in p1s]
                x = ht_r[rows, ls_]
                act = (0.5 * x * (1.0 + lax.erf(x * (2.0 ** -0.5)))).astype(BF16)
                for rg in range(nk // pack):
                    rs = slice(rg * pack, (rg + 1) * pack)
                    g = jnp.zeros((pack, LANES), BF16)
                    for h in range(PEER_HEADS):
                        g = g + jnp.where(r2_ref[h, rs, ls_] < c1b[h], p2_ref[h, rs, ls_] * p1b[h],
                                          jnp.zeros((), BF16))
                    w_ref[ii * nk + rg * pack:ii * nk + (rg + 1) * pack, ls_] = act[rs, :] * g
        yt_ref[...] += jnp.dot(vt_ref[...], w_ref[...], preferred_element_type=F32)

    odd = lax.rem(e, 2) == 1

    @pl.when(odd)
    def _():
        main_step(htb_ref, hta_ref)

    @pl.when(jnp.logical_and(e > 0, jnp.logical_not(odd)))
    def _():
        main_step(hta_ref, htb_ref)

    @pl.when(e == n_steps - 1)
    def _():
        o_ref[...] = yt_ref[...].T


def _peer_module(xn2, q2, k1, k2, u, vt_tiles, tt):
    t, d = xn2.shape
    ne, _, te = vt_tiles.shape
    nk = PEER_NKEYS
    sel32 = pltpu.VMEM((PEER_HEADS, nk, tt), F32)
    sel16 = pltpu.VMEM((PEER_HEADS, nk, tt), BF16)
    return pl.pallas_call(
        _peer_kernel,
        grid=(t // tt, ne + 1),
        in_specs=[
            pl.BlockSpec((tt, d), lambda i, e: (i, 0)),
            pl.BlockSpec((tt, d), lambda i, e: (i, 0)),
            pl.BlockSpec((PEER_HEADS, nk, nk), lambda i, e: (0, 0, 0)),
            pl.BlockSpec((PEER_HEADS, nk, nk), lambda i, e: (0, 0, 0)),
            pl.BlockSpec((te, d), lambda i, e: (jnp.minimum(e, ne - 1), 0)),
            pl.BlockSpec((None, d, te), lambda i, e: (jnp.maximum(e - 1, 0), 0, 0)),
        ],
        out_specs=pl.BlockSpec((tt, d), lambda i, e: (i, 0)),
        out_shape=jax.ShapeDtypeStruct((t, d), F32),
        scratch_shapes=[
            pltpu.VMEM((d, tt), F32),
            pltpu.VMEM((te, tt), F32), pltpu.VMEM((te, tt), F32),
            pltpu.VMEM((te, tt), BF16),
            pltpu.VMEM((nk, tt), F32), pltpu.VMEM((nk, tt), F32),
            sel32, sel32, sel16, sel16,
        ],
        compiler_params=_params("parallel", "arbitrary"),
        name="peer_module",
    )(xn2, q2, k1, k2, u, vt_tiles)


def _final_kernel(h_ref, y_ref, g_ref, o_ref):
    h = h_ref[...] + y_ref[...]
    ms = jnp.mean(h * h, axis=-1, keepdims=True)
    o_ref[...] = h * lax.rsqrt(ms + EPS) * g_ref[...]


def _final_module(h1, y, g, tm):
    t, d = h1.shape
    row = lambda i: (i, 0)
    return pl.pallas_call(
        _final_kernel,
        grid=(t // tm,),
        in_specs=[pl.BlockSpec((tm, d), row), pl.BlockSpec((tm, d), row), pl.BlockSpec((1, d), lambda i: (0, 0))],
        out_specs=pl.BlockSpec((tm, d), row),
        out_shape=jax.ShapeDtypeStruct((t, d), F32),
        compiler_params=_params("parallel"),
        name="final_norm",
    )(h1, y, g)


def _layer(x2d, xm, nb, ls, norm1_g, w_in, w_alpha2, b_alpha, conv_w, conv_b, conv_ln_g, conv_ln_b,
           w_conv_out, gla_norm_g, w_gla_out, w_out, norm2_g, peer_wq, peer_k1, peer_k2, peer_u, peer_v):
    d = x2d.shape[1]
    n_pre = 5 * d
    rank = w_alpha2.shape[0]
    w_main = jnp.concatenate([w_in[:, :n_pre], w_in[:, n_pre + rank:]], axis=1).astype(BF16)
    w_tiles = w_main.reshape(d, -1, INPROJ_TN).transpose(1, 0, 2)
    w_a = jnp.pad(w_in[:, n_pre:n_pre + rank], ((0, 0), (0, LANES - rank))).astype(BF16)
    g1 = norm1_g.reshape(1, d)

    tm = min(1024, x2d.shape[0])
    proj, a_x = _in_proj(x2d, g1, w_tiles, w_a, tm)
    proj_m, a_m = _in_proj(xm, g1, w_tiles, w_a, META_TILE)

    cw = jnp.pad(conv_w, ((0, CONV_HALO - CONV_WIDTH), (0, 0)))
    yc = _conv_module(proj, proj_m, cw, conv_b.reshape(1, d), conv_ln_g.reshape(1, d), conv_ln_b.reshape(1, d),
                      w_conv_out.astype(BF16), nb, ls, d, 256)

    wa2 = jnp.pad(w_alpha2, ((0, LANES - rank), (0, 0))).astype(BF16)
    yg = _gla_module(proj, a_x, proj_m, a_m, wa2, b_alpha.reshape(1, -1), gla_norm_g.reshape(1, d),
                     w_gla_out.astype(BF16), nb, ls, d, 256)

    h1, xn2, q2 = _mix_module(yc, yg, x2d, w_out.astype(BF16), norm2_g.reshape(1, d), peer_wq.astype(BF16), 256)

    vt_tiles = peer_v.astype(BF16).reshape(-1, PEER_TE, d).transpose(0, 2, 1)
    y = _peer_module(xn2, q2, peer_k1.astype(BF16), peer_k2.astype(BF16), peer_u.astype(BF16),
                     vt_tiles, min(512, x2d.shape[0]))
    return h1, y


def kernel(x, meta_tokens, norm1_g, w_in, w_alpha2, b_alpha, conv_w, conv_b, conv_ln_g, conv_ln_b, w_conv_out,
           gla_norm_g, w_gla_out, w_out, norm2_g, peer_wq, peer_k1, peer_k2, peer_u, peer_v, normf_g):
    nb, ls, d = x.shape
    depth = w_in.shape[0]
    assert depth == 1, "meta-token positions are only carried through one layer"
    x2d = x.reshape(nb * ls, d)
    xm = jnp.concatenate([jnp.zeros((META_TILE - N_META, d), x.dtype), meta_tokens.astype(x.dtype)], axis=0)
    h1, y = _layer(x2d, xm, nb, ls, norm1_g[0], w_in[0], w_alpha2[0], b_alpha[0], conv_w[0], conv_b[0],
                   conv_ln_g[0], conv_ln_b[0], w_conv_out[0], gla_norm_g[0], w_gla_out[0], w_out[0],
                   norm2_g[0], peer_wq[0], peer_k1[0], peer_k2[0], peer_u[0], peer_v[0])
    out = _final_module(h1, y, normf_g.reshape(1, d), 256)
    return out.reshape(nb, ls, d)
```

```python
import functools

import jax
import jax.numpy as jnp
from jax import lax
from jax.experimental import pallas as pl
from jax.experimental.pallas import tpu as pltpu

F32 = jnp.float32
BF16 = jnp.bfloat16

N_META = 16
CONV_WIDTH = 31
GLA_HEADS = 4
GLA_TAU = 16.0
GLA_CHUNK = 64
PEER_HEADS = 8
PEER_NKEYS = 128
PEER_TOPK = 16
EPS = 1e-6

LANES = 128
META_TILE = 256
CONV_HALO = 32
INPROJ_TN = 1024
PEER_TE = 1024
VMEM_LIMIT_BYTES = 56 * 1024 * 1024

_NT = (((1,), (1,)), ((), ()))
_TN = (((0,), (0,)), ((), ()))


def _sigmoid(x):
    return 1.0 / (1.0 + jnp.exp(-x))


def _params(*sem):
    return pltpu.CompilerParams(dimension_semantics=sem, vmem_limit_bytes=VMEM_LIMIT_BYTES)


def _inproj_kernel(x_ref, g_ref, w_ref, wa_ref, o_ref, a_ref, xn_ref):
    @pl.when(pl.program_id(1) == 0)
    def _():
        rows = 128
        def body(r, _):
            r0 = pl.multiple_of(r * rows, rows)
            x = x_ref[pl.ds(r0, rows), :]
            ms = jnp.mean(x * x, axis=-1, keepdims=True)
            xn_ref[pl.ds(r0, rows), :] = (x * lax.rsqrt(ms + EPS) * g_ref[...]).astype(BF16)
            return 0
        lax.fori_loop(0, x_ref.shape[0] // rows, body, 0)
        a_ref[...] = jnp.dot(xn_ref[...], wa_ref[...], preferred_element_type=F32)

    o_ref[...] = jnp.dot(xn_ref[...], w_ref[...], preferred_element_type=F32)


def _in_proj(x2d, g, w_tiles, w_a, tm):
    t, d = x2d.shape
    n_tiles, _, tn = w_tiles.shape
    n = n_tiles * tn
    return pl.pallas_call(
        _inproj_kernel,
        grid=(t // tm, n_tiles),
        in_specs=[
            pl.BlockSpec((tm, d), lambda i, j: (i, 0)),
            pl.BlockSpec((1, d), lambda i, j: (0, 0)),
            pl.BlockSpec((None, d, tn), lambda i, j: (j, 0, 0)),
            pl.BlockSpec((d, LANES), lambda i, j: (0, 0)),
        ],
        out_specs=[
            pl.BlockSpec((tm, tn), lambda i, j: (i, j)),
            pl.BlockSpec((tm, LANES), lambda i, j: (i, 0)),
        ],
        out_shape=[jax.ShapeDtypeStruct((t, n), F32), jax.ShapeDtypeStruct((t, LANES), F32)],
        scratch_shapes=[pltpu.VMEM((tm, d), BF16)],
        compiler_params=_params("parallel", "arbitrary"),
        name="in_proj",
    )(x2d, g, w_tiles, w_a)


def _conv_kernel(a1_ref, a2_ref, p1_ref, p2_ref, m1_ref, m2_ref, cw_ref, cb_ref, lg_ref, lb_ref,
                 wo_ref, gm_ref, o_ref, hbuf_ref, cbuf_ref, hn_ref):
    tc, ch = a1_ref.shape
    first_tile = pl.program_id(1) == 0

    for r0 in range(0, tc, 64):
        hbuf_ref[CONV_HALO + r0:CONV_HALO + r0 + 64, :] = (
            a1_ref[r0:r0 + 64, :] * _sigmoid(a2_ref[r0:r0 + 64, :]))

    @pl.when(first_tile)
    def _():
        hbuf_ref[0:CONV_HALO, :] = m1_ref[...] * _sigmoid(m2_ref[...])

    @pl.when(jnp.logical_not(first_tile))
    def _():
        hbuf_ref[0:CONV_HALO, :] = p1_ref[...] * _sigmoid(p2_ref[...])

    sub = 8
    rb = 128
    for c0 in range(0, ch, LANES):
        lanes = slice(c0, c0 + LANES)
        for t0 in range(0, tc, rb):
            acc = jnp.zeros((rb, LANES), F32) + cb_ref[:, lanes]
            for s in range(sub):
                lead = 0 if s == 0 else sub
                part = None
                for a in range((CONV_WIDTH - 1 - s) // sub + 1):
                    j = CONV_WIDTH - 1 - (sub * a + s)
                    r0 = CONV_HALO + t0 - lead - sub * a
                    term = hbuf_ref[r0:r0 + rb + lead, lanes] * cw_ref[j:j + 1, lanes]
                    part = term if part is None else part + term
                acc = acc + part[lead - s:lead - s + rb, :]
            cbuf_ref[t0:t0 + rb, lanes] = acc

    rows = 32
    def ln_body(r, _):
        r0 = pl.multiple_of(r * rows, rows)
        x = cbuf_ref[pl.ds(r0, rows), :]
        mu = jnp.mean(x, axis=-1, keepdims=True)
        xc = x - mu
        var = jnp.mean(xc * xc, axis=-1, keepdims=True)
        y = xc * lax.rsqrt(var + EPS) * lg_ref[...] + lb_ref[...]
        hn_ref[pl.ds(r0, rows), :] = (y * _sigmoid(y)).astype(BF16)
        return 0
    lax.fori_loop(0, tc // rows, ln_body, 0)

    y = jnp.dot(hn_ref[...], wo_ref[...], preferred_element_type=F32)
    o_ref[...] = y * _sigmoid(gm_ref[...])


def _conv_module(proj, proj_m, conv_w, conv_b, ln_g, ln_b, w_out, nb, ls, ch, tc):
    nt = ls // tc
    hpt = tc // CONV_HALO
    meta_blk = META_TILE // CONV_HALO - 1
    merge_blk = 10240 // ch

    def cur(col):
        return lambda b, i: (b * nt + i, col)

    def prev(col):
        return lambda b, i: (jnp.maximum((b * nt + i) * hpt - 1, 0), col)

    const = lambda b, i: (0, 0)
    return pl.pallas_call(
        _conv_kernel,
        grid=(nb, nt),
        in_specs=[
            pl.BlockSpec((tc, ch), cur(0)),
            pl.BlockSpec((tc, ch), cur(1)),
            pl.BlockSpec((CONV_HALO, ch), prev(0)),
            pl.BlockSpec((CONV_HALO, ch), prev(1)),
            pl.BlockSpec((CONV_HALO, ch), lambda b, i: (meta_blk, 0)),
            pl.BlockSpec((CONV_HALO, ch), lambda b, i: (meta_blk, 1)),
            pl.BlockSpec((CONV_HALO, ch), const),
            pl.BlockSpec((1, ch), const),
            pl.BlockSpec((1, ch), const),
            pl.BlockSpec((1, ch), const),
            pl.BlockSpec((ch, ch), const),
            pl.BlockSpec((tc, ch), cur(merge_blk)),
        ],
        out_specs=pl.BlockSpec((tc, ch), cur(0)),
        out_shape=jax.ShapeDtypeStruct((nb * ls, ch), F32),
        scratch_shapes=[
            pltpu.VMEM((CONV_HALO + tc, ch), F32),
            pltpu.VMEM((tc, ch), F32),
            pltpu.VMEM((tc, ch), BF16),
        ],
        compiler_params=_params("parallel", "arbitrary"),
        name="conv_module",
    )(proj, proj, proj, proj, proj_m, proj_m, conv_w, conv_b, ln_g, ln_b, w_out, proj)


def _gla_kernel(q_ref, k_ref, v_ref, r_ref, gm_ref, a_ref, km_ref, vm_ref, am_ref, wa2_ref, ba_ref,
                ng_ref, wo_ref, o_ref, st_ref, ob_ref):
    nh = GLA_HEADS
    c = GLA_CHUNK
    tb = q_ref.shape[0]
    dk = q_ref.shape[1] // nh
    dv = v_ref.shape[1] // nh
    scale = dk ** -0.5

    row = lax.broadcasted_iota(jnp.int32, (c, c), 0)
    col = lax.broadcasted_iota(jnp.int32, (c, c), 1)
    tri = (col <= row).astype(F32)

    def log_decay(a_c):
        z = jnp.dot(a_c.astype(BF16), wa2_ref[...], preferred_element_type=F32) + ba_ref[...]
        return (jnp.minimum(z, 0.0) - jnp.log1p(jnp.exp(-jnp.abs(z)))) * (1.0 / GLA_TAU)

    def cumsum_rows(g):
        return jnp.dot(tri, g, preferred_element_type=F32, precision=lax.Precision.HIGHEST)

    @pl.when(pl.program_id(1) == 0)
    def _():
        g = log_decay(am_ref[...])
        ridx = lax.broadcasted_iota(jnp.int32, (c, 1), 0)
        g = jnp.where(ridx >= c - N_META, g, 0.0)
        b = cumsum_rows(g)
        bl = b[c - 1:c, :]
        for h in range(nh):
            ks = slice(h * dk, (h + 1) * dk)
            ke = km_ref[:, ks] * jnp.exp(bl[:, ks] - b[:, ks])
            vh = vm_ref[:, h * dv:(h + 1) * dv]
            st_ref[h] = lax.dot_general(vh.astype(BF16), ke.astype(BF16), _TN, preferred_element_type=F32)

    levels = (32, 16, 8)
    level_masks = []
    for s in levels:
        level_masks.append((row // (2 * s) == col // (2 * s)) & ((row // s) % 2 == 1) & ((col // s) % 2 == 0))
    sub = lax.broadcasted_iota(jnp.int32, (8, c), 0)
    lane = lax.broadcasted_iota(jnp.int32, (8, c), 1)

    def chunk_body(ci, carry):
        r0 = pl.multiple_of(ci * c, c)
        g_all = log_decay(a_ref[pl.ds(r0, c), :])
        b_all = cumsum_rows(g_all)
        for h in range(nh):
            ks = slice(h * dk, (h + 1) * dk)
            vs = slice(h * dv, (h + 1) * dv)
            b = b_all[:, ks]
            q = q_ref[pl.ds(r0, c), ks] * scale
            k = k_ref[pl.ds(r0, c), ks]
            vb = v_ref[pl.ds(r0, c), vs].astype(BF16)
            bl = b[c - 1:c, :]
            st = st_ref[h]

            o = lax.dot_general((q * jnp.exp(b)).astype(BF16), st.astype(BF16), _NT,
                                preferred_element_type=F32)

            att = jnp.zeros((c, c), F32)
            for s, msk in zip(levels, level_masks):
                pieces = []
                for p in range(c // (2 * s)):
                    rr = p * 2 * s + s - 1
                    pieces.append(jnp.broadcast_to(b[rr:rr + 1, :], (2 * s, dk)))
                bref = pieces[0] if len(pieces) == 1 else jnp.concatenate(pieces, axis=0)
                qs = q * jnp.exp(jnp.minimum(b - bref, 0.0))
                kk = k * jnp.exp(jnp.minimum(bref - b, 0.0))
                pm = lax.dot_general(qs.astype(BF16), kk.astype(BF16), _NT, preferred_element_type=F32)
                att = att + jnp.where(msk, pm, 0.0)

            blocks = []
            for bi in range(c // 8):
                qb = q[bi * 8:(bi + 1) * 8, :]
                bb = b[bi * 8:(bi + 1) * 8, :]
                blk = jnp.zeros((8, c), F32)
                for jj in range(8):
                    j = bi * 8 + jj
                    e = jnp.exp(jnp.minimum(bb - b[j:j + 1, :], 0.0))
                    colv = jnp.sum(qb * (k[j:j + 1, :] * e), axis=-1, keepdims=True)
                    blk = blk + jnp.where((lane == j) & (sub >= jj), colv, 0.0)
                blocks.append(blk)
            att = att + jnp.concatenate(blocks, axis=0)

            o = o + jnp.dot(att.astype(BF16), vb, preferred_element_type=F32)

            ke = k * jnp.exp(bl - b)
            st_ref[h] = st * jnp.exp(bl) + lax.dot_general(vb, ke.astype(BF16), _TN,
                                                            preferred_element_type=F32)

            ms = jnp.mean(o * o, axis=-1, keepdims=True)
            rr_ = r_ref[pl.ds(r0, c), vs]
            on = o * lax.rsqrt(ms + EPS) * ng_ref[:, vs] * (rr_ * _sigmoid(rr_))
            ob_ref[pl.ds(r0, c), vs] = on.astype(BF16)
        return carry

    lax.fori_loop(0, tb // c, chunk_body, 0)

    y = jnp.dot(ob_ref[...], wo_ref[...], preferred_element_type=F32)
    o_ref[...] = y * _sigmoid(gm_ref[...])


def _gla_module(proj, a_x, proj_m, a_m, wa2, ba, ng, w_out, nb, ls, d, tb):
    nt = ls // tb
    nqk = d // 2
    c = GLA_CHUNK
    mrow = META_TILE // c - 1

    def cur(col):
        return lambda b, i: (b * nt + i, col)

    const = lambda b, i: (0, 0)
    return pl.pallas_call(
        _gla_kernel,
        grid=(nb, nt),
        in_specs=[
            pl.BlockSpec((tb, nqk), cur(4096 // nqk)),
            pl.BlockSpec((tb, nqk), cur(4096 // nqk + 1)),
            pl.BlockSpec((tb, d), cur(6144 // d)),
            pl.BlockSpec((tb, d), cur(8192 // d)),
            pl.BlockSpec((tb, d), cur(12288 // d)),
            pl.BlockSpec((tb, LANES), cur(0)),
            pl.BlockSpec((c, nqk), lambda b, i: (mrow, 4096 // nqk + 1)),
            pl.BlockSpec((c, d), lambda b, i: (mrow, 6144 // d)),
            pl.BlockSpec((c, LANES), lambda b, i: (mrow, 0)),
            pl.BlockSpec((LANES, nqk), const),
            pl.BlockSpec((1, nqk), const),
            pl.BlockSpec((1, d), const),
            pl.BlockSpec((d, d), const),
        ],
        out_specs=pl.BlockSpec((tb, d), cur(0)),
        out_shape=jax.ShapeDtypeStruct((nb * ls, d), F32),
        scratch_shapes=[
            pltpu.VMEM((GLA_HEADS, d // GLA_HEADS, nqk // GLA_HEADS), F32),
            pltpu.VMEM((tb, d), BF16),
        ],
        compiler_params=_params("parallel", "arbitrary"),
        name="gla_module",
    )(proj, proj, proj, proj, proj, a_x, proj_m, proj_m, a_m, wa2, ba, ng, w_out)


def _mix_kernel(yc_ref, yg_ref, x_ref, wo_ref, g2_ref, wq_ref, h1_ref, xn2_ref, q2_ref):
    m = (yc_ref[...] + yg_ref[...]).astype(BF16)
    h1 = x_ref[...] + jnp.dot(m, wo_ref[...], preferred_element_type=F32)
    h1_ref[...] = h1
    ms = jnp.mean(h1 * h1, axis=-1, keepdims=True)
    xn2 = (h1 * lax.rsqrt(ms + EPS) * g2_ref[...]).astype(BF16)
    xn2_ref[...] = xn2
    q2_ref[...] = jnp.dot(xn2, wq_ref[...], preferred_element_type=F32).astype(BF16)


def _mix_module(yc, yg, x2d, w_out, g2, wq, tm):
    t, d = x2d.shape
    row = lambda i: (i, 0)
    const = lambda i: (0, 0)
    return pl.pallas_call(
        _mix_kernel,
        grid=(t // tm,),
        in_specs=[
            pl.BlockSpec((tm, d), row), pl.BlockSpec((tm, d), row), pl.BlockSpec((tm, d), row),
            pl.BlockSpec((d, d), const), pl.BlockSpec((1, d), const), pl.BlockSpec((d, d), const),
        ],
        out_specs=[pl.BlockSpec((tm, d), row), pl.BlockSpec((tm, d), row), pl.BlockSpec((tm, d), row)],
        out_shape=[jax.ShapeDtypeStruct((t, d), F32), jax.ShapeDtypeStruct((t, d), BF16),
                   jax.ShapeDtypeStruct((t, d), BF16)],
        compiler_params=_params("parallel"),
        name="mix_module",
    )(yc, yg, x2d, w_out, g2, wq)


_NO_RANK = 100.0


def _top16(s):
    idx = lax.broadcasted_iota(jnp.int32, s.shape, 0).astype(F32)
    rank = jnp.full(s.shape, _NO_RANK, F32)
    vals = []
    for r in range(PEER_TOPK):
        m = jnp.max(s, axis=0, keepdims=True)
        first = jnp.min(jnp.where(s == m, idx, 1e9), axis=0, keepdims=True)
        sel = idx == first
        rank = jnp.where(sel, float(r), rank)
        s = jnp.where(sel, -jnp.inf, s)
        vals.append(m)
    return rank, jnp.concatenate(vals, axis=0)


def _top16_distinct(s):
    rank = jnp.full(s.shape, _NO_RANK, F32)
    vals = []
    for r in range(PEER_TOPK):
        m = jnp.max(s, axis=0, keepdims=True)
        sel = s == m
        rank = jnp.where(sel, float(r), rank)
        s = jnp.where(sel, -jnp.inf, s)
        vals.append(m)
    n_ranked = jnp.sum(jnp.where(rank < float(PEER_TOPK), 1.0, 0.0), axis=0, keepdims=True)
    return rank, jnp.concatenate(vals, axis=0), n_ranked


def _frontier_counts(v1, v2):
    kk = PEER_TOPK
    r1 = lax.broadcasted_iota(jnp.int32, v1.shape, 0).astype(F32)
    cnt = jnp.zeros(v1.shape, F32)
    front = v1 + v2[0:1, :]
    for it in range(kk):
        m = jnp.max(front, axis=0, keepdims=True)
        first = jnp.min(jnp.where(front == m, r1, 99.0), axis=0, keepdims=True)
        sel = r1 == first
        cnt = cnt + sel.astype(F32)
        if it + 1 < kk:
            c_sel = jnp.sum(jnp.where(sel, cnt, 0.0), axis=0, keepdims=True)
            nxt = jnp.sum(jnp.where(r1 == c_sel, v2, 0.0), axis=0, keepdims=True)
            nxt = jnp.where(c_sel >= float(kk), -jnp.inf, nxt)
            front = jnp.where(sel, v1 + nxt, front)
    return cnt


def _peer_kernel(xn_ref, q_ref, k1_ref, k2_ref, u_ref, vt_ref, o_ref,
                 yt_ref, hta_ref, htb_ref, w_ref, s1_ref, s2_ref, p1_ref, c1_ref, r2_ref, p2_ref):
    e = pl.program_id(1)
    n_steps = pl.num_programs(1)
    tt = xn_ref.shape[0]
    te = u_ref.shape[0]
    nk = PEER_NKEYS
    kk = PEER_TOPK
    n_i1 = te // nk
    pack = 16

    def expert_acts():
        return lax.dot_general(u_ref[...], xn_ref[...], _NT, preferred_element_type=F32)

    @pl.when(e == 0)
    def _():
        yt_ref[...] = jnp.zeros_like(yt_ref)
        hta_ref[...] = expert_acts()

        def head_body(h, carry):
            c0 = pl.multiple_of(h * 2 * nk, 2 * nk)
            s1_ref[...] = lax.dot_general(k1_ref[h], q_ref[:, pl.ds(c0, nk)], _NT,
                                          preferred_element_type=F32)
            s2_ref[...] = lax.dot_general(k2_ref[h], q_ref[:, pl.ds(c0 + nk, nk)], _NT,
                                          preferred_element_type=F32)

            def lane_body(lt, carry2):
                l0 = pl.multiple_of(lt * LANES, LANES)
                s1 = s1_ref[:, pl.ds(l0, LANES)]
                s2 = s2_ref[:, pl.ds(l0, LANES)]

                def route(rank1, v1, rank2, v2):
                    cnt = _frontier_counts(v1, v2)
                    p1v = jnp.exp(v1 - v1[0:1, :])
                    p2v = jnp.exp(v2 - v2[0:1, :])
                    cum = jnp.zeros(v1.shape, F32)
                    for r2 in range(kk):
                        cum = cum + jnp.where(cnt > float(r2), p2v[r2:r2 + 1, :], 0.0)
                    z = jnp.sum(p1v * cum, axis=0, keepdims=True)
                    c1 = jnp.zeros(s1.shape, F32)
                    for r in range(kk):
                        c1 = jnp.where(rank1 == float(r), cnt[r:r + 1, :], c1)
                    p1_ref[h, :, pl.ds(l0, LANES)] = jnp.exp(s1 - v1[0:1, :]) / z
                    c1_ref[h, :, pl.ds(l0, LANES)] = c1
                    r2_ref[h, :, pl.ds(l0, LANES)] = rank2.astype(BF16)
                    p2_ref[h, :, pl.ds(l0, LANES)] = jnp.exp(s2 - v2[0:1, :]).astype(BF16)

                rank1, v1, n1 = _top16_distinct(s1)
                rank2, v2, n2 = _top16_distinct(s2)
                route(rank1, v1, rank2, v2)
                ties = jnp.max(jnp.abs(n1 - float(kk)) + jnp.abs(n2 - float(kk)))

                @pl.when(ties > 0.0)
                def _():
                    route(*_top16(s1), *_top16(s2))

                return carry2

            lax.fori_loop(0, tt // LANES, lane_body, 0)
            return carry

        lax.fori_loop(0, PEER_HEADS, head_body, 0)

    def main_step(ht_w, ht_r):
        ht_w[...] = expert_acts()
        i1_base = pl.multiple_of((e - 1) * n_i1, n_i1)
        for lt in range(tt // LANES):
            ls_ = slice(lt * LANES, (lt + 1) * LANES)
            c1s = [c1_ref[h, pl.ds(i1_base, n_i1), ls_] for h in range(PEER_HEADS)]
            p1s = [p1_ref[h, pl.ds(i1_base, n_i1), ls_] for h in range(PEER_HEADS)]
            for ii in range(n_i1):
                rows = slice(ii * nk, (ii + 1) * nk)
                c1b = [jnp.broadcast_to(c[ii:ii + 1, :], (pack, LANES)).astype(BF16) for c in c1s]
                p1b = [jnp.broadcast_to(p[ii:ii + 1, :], (pack, LANES)).astype(BF16) for p in p1s]
                x = ht_r[rows, ls_]
                act = (0.5 * x * (1.0 + lax.erf(x * (2.0 ** -0.5)))).astype(BF16)
                for rg in range(nk // pack):
                    rs = slice(rg * pack, (rg + 1) * pack)
                    g = jnp.zeros((pack, LANES), BF16)
                    for h in range(PEER_HEADS):
                        m = jnp.minimum(jnp.maximum(c1b[h] - r2_ref[h, rs, ls_], jnp.zeros((), BF16)), p1b[h])
                        g = g + m * p2_ref[h, rs, ls_]
                    w_ref[ii * nk + rg * pack:ii * nk + (rg + 1) * pack, ls_] = act[rs, :] * g
        yt_ref[...] += jnp.dot(vt_ref[...], w_ref[...], preferred_element_type=F32)

    odd = lax.rem(e, 2) == 1

    @pl.when(odd)
    def _():
        main_step(htb_ref, hta_ref)

    @pl.when(jnp.logical_and(e > 0, jnp.logical_not(odd)))
    def _():
        main_step(hta_ref, htb_ref)

    @pl.when(e == n_steps - 1)
    def _():
        o_ref[...] = yt_ref[...].T


def _peer_module(xn2, q2, k1, k2, u, vt_tiles, tt):
    t, d = xn2.shape
    ne, _, te = vt_tiles.shape
    nk = PEER_NKEYS
    sel32 = pltpu.VMEM((PEER_HEADS, nk, tt), F32)
    sel16 = pltpu.VMEM((PEER_HEADS, nk, tt), BF16)
    return pl.pallas_call(
        _peer_kernel,
        grid=(t // tt, ne + 1),
        in_specs=[
            pl.BlockSpec((tt, d), lambda i, e: (i, 0)),
            pl.BlockSpec((tt, d), lambda i, e: (i, 0)),
            pl.BlockSpec((PEER_HEADS, nk, nk), lambda i, e: (0, 0, 0)),
            pl.BlockSpec((PEER_HEADS, nk, nk), lambda i, e: (0, 0, 0)),
            pl.BlockSpec((te, d), lambda i, e: (jnp.minimum(e, ne - 1), 0)),
            pl.BlockSpec((None, d, te), lambda i, e: (jnp.maximum(e - 1, 0), 0, 0)),
        ],
        out_specs=pl.BlockSpec((tt, d), lambda i, e: (i, 0)),
        out_shape=jax.ShapeDtypeStruct((t, d), F32),
        scratch_shapes=[
            pltpu.VMEM((d, tt), F32),
            pltpu.VMEM((te, tt), F32), pltpu.VMEM((te, tt), F32),
            pltpu.VMEM((te, tt), BF16),
            pltpu.VMEM((nk, tt), F32), pltpu.VMEM((nk, tt), F32),
            sel32, sel32, sel16, sel16,
        ],
        compiler_params=_params("parallel", "arbitrary"),
        name="peer_module",
    )(xn2, q2, k1, k2, u, vt_tiles)


def _final_kernel(h_ref, y_ref, g_ref, o_ref):
    h = h_ref[...] + y_ref[...]
    ms = jnp.mean(h * h, axis=-1, keepdims=True)
    o_ref[...] = h * lax.rsqrt(ms + EPS) * g_ref[...]


def _final_module(h1, y, g, tm):
    t, d = h1.shape
    row = lambda i: (i, 0)
    return pl.pallas_call(
        _final_kernel,
        grid=(t // tm,),
        in_specs=[pl.BlockSpec((tm, d), row), pl.BlockSpec((tm, d), row), pl.BlockSpec((1, d), lambda i: (0, 0))],
        out_specs=pl.BlockSpec((tm, d), row),
        out_shape=jax.ShapeDtypeStruct((t, d), F32),
        compiler_params=_params("parallel"),
        name="final_norm",
    )(h1, y, g)


def _layer(x2d, xm, nb, ls, norm1_g, w_in, w_alpha2, b_alpha, conv_w, conv_b, conv_ln_g, conv_ln_b,
           w_conv_out, gla_norm_g, w_gla_out, w_out, norm2_g, peer_wq, peer_k1, peer_k2, peer_u, peer_v):
    d = x2d.shape[1]
    n_pre = 5 * d
    rank = w_alpha2.shape[0]
    w_main = jnp.concatenate([w_in[:, :n_pre], w_in[:, n_pre + rank:]], axis=1).astype(BF16)
    w_tiles = w_main.reshape(d, -1, INPROJ_TN).transpose(1, 0, 2)
    w_a = jnp.pad(w_in[:, n_pre:n_pre + rank], ((0, 0), (0, LANES - rank))).astype(BF16)
    g1 = norm1_g.reshape(1, d)

    tm = min(1024, x2d.shape[0])
    proj, a_x = _in_proj(x2d, g1, w_tiles, w_a, tm)
    proj_m, a_m = _in_proj(xm, g1, w_tiles, w_a, META_TILE)

    cw = jnp.pad(conv_w, ((0, CONV_HALO - CONV_WIDTH), (0, 0)))
    yc = _conv_module(proj, proj_m, cw, conv_b.reshape(1, d), conv_ln_g.reshape(1, d), conv_ln_b.reshape(1, d),
                      w_conv_out.astype(BF16), nb, ls, d, 256)

    wa2 = jnp.pad(w_alpha2, ((0, LANES - rank), (0, 0))).astype(BF16)
    yg = _gla_module(proj, a_x, proj_m, a_m, wa2, b_alpha.reshape(1, -1), gla_norm_g.reshape(1, d),
                     w_gla_out.astype(BF16), nb, ls, d, 256)

    h1, xn2, q2 = _mix_module(yc, yg, x2d, w_out.astype(BF16), norm2_g.reshape(1, d), peer_wq.astype(BF16), 256)

    vt_tiles = peer_v.astype(BF16).reshape(-1, PEER_TE, d).transpose(0, 2, 1)
    y = _peer_module(xn2, q2, peer_k1.astype(BF16), peer_k2.astype(BF16), peer_u.astype(BF16),
                     vt_tiles, min(512, x2d.shape[0]))
    return h1, y


def kernel(x, meta_tokens, norm1_g, w_in, w_alpha2, b_alpha, conv_w, conv_b, conv_ln_g, conv_ln_b, w_conv_out,
           gla_norm_g, w_gla_out, w_out, norm2_g, peer_wq, peer_k1, peer_k2, peer_u, peer_v, normf_g):
    nb, ls, d = x.shape
    depth = w_in.shape[0]
    assert depth == 1, "meta-token positions are only carried through one layer"
    x2d = x.reshape(nb * ls, d)
    xm = jnp.concatenate([jnp.zeros((META_TILE - N_META, d), x.dtype), meta_tokens.astype(x.dtype)], axis=0)
    h1, y = _layer(x2d, xm, nb, ls, norm1_g[0], w_in[0], w_alpha2[0], b_alpha[0], conv_w[0], conv_b[0],
                   conv_ln_g[0], conv_ln_b[0], w_conv_out[0], gla_norm_g[0], w_gla_out[0], w_out[0],
                   norm2_g[0], peer_wq[0], peer_k1[0], peer_k2[0], peer_u[0], peer_v[0])
    out = _final_module(h1, y, normf_g.reshape(1, d), 256)
    return out.reshape(nb, ls, d)
```

```python
import functools

import jax
import jax.numpy as jnp
from jax import lax
from jax.experimental import pallas as pl
from jax.experimental.pallas import tpu as pltpu

F32 = jnp.float32
BF16 = jnp.bfloat16

N_META = 16
CONV_WIDTH = 31
GLA_HEADS = 4
GLA_TAU = 16.0
GLA_CHUNK = 64
PEER_HEADS = 8
PEER_NKEYS = 128
PEER_TOPK = 16
EPS = 1e-6

LANES = 128
META_TILE = 256
CONV_HALO = 32
INPROJ_TN = 1024
PEER_TE = 1024
VMEM_LIMIT_BYTES = 60 * 1024 * 1024

_NT = (((1,), (1,)), ((), ()))
_TN = (((0,), (0,)), ((), ()))


def _sigmoid(x):
    return 1.0 / (1.0 + jnp.exp(-x))


def _params(*sem):
    return pltpu.CompilerParams(dimension_semantics=sem, vmem_limit_bytes=VMEM_LIMIT_BYTES)


def _inproj_kernel(x_ref, g_ref, w_ref, wa_ref, o_ref, a_ref, xn_ref):
    @pl.when(pl.program_id(1) == 0)
    def _():
        rows = 128
        def body(r, _):
            r0 = pl.multiple_of(r * rows, rows)
            x = x_ref[pl.ds(r0, rows), :]
            ms = jnp.mean(x * x, axis=-1, keepdims=True)
            xn_ref[pl.ds(r0, rows), :] = (x * lax.rsqrt(ms + EPS) * g_ref[...]).astype(BF16)
            return 0
        lax.fori_loop(0, x_ref.shape[0] // rows, body, 0)
        a_ref[...] = jnp.dot(xn_ref[...], wa_ref[...], preferred_element_type=F32)

    o_ref[...] = jnp.dot(xn_ref[...], w_ref[...], preferred_element_type=F32)


def _in_proj(x2d, g, w_main, w_a, tm, tn):
    t, d = x2d.shape
    n = w_main.shape[1]
    return pl.pallas_call(
        _inproj_kernel,
        grid=(t // tm, n // tn),
        in_specs=[
            pl.BlockSpec((tm, d), lambda i, j: (i, 0)),
            pl.BlockSpec((1, d), lambda i, j: (0, 0)),
            pl.BlockSpec((d, tn), lambda i, j: (0, j)),
            pl.BlockSpec((d, LANES), lambda i, j: (0, 0)),
        ],
        out_specs=[
            pl.BlockSpec((tm, tn), lambda i, j: (i, j)),
            pl.BlockSpec((tm, LANES), lambda i, j: (i, 0)),
        ],
        out_shape=[jax.ShapeDtypeStruct((t, n), F32), jax.ShapeDtypeStruct((t, LANES), F32)],
        scratch_shapes=[pltpu.VMEM((tm, d), BF16)],
        compiler_params=_params("parallel", "arbitrary"),
        name="in_proj",
    )(x2d, g, w_main, w_a)


def _conv_kernel(a1_ref, a2_ref, p1_ref, p2_ref, m1_ref, m2_ref, cw_ref, cb_ref, lg_ref, lb_ref,
                 wo_ref, gm_ref, o_ref, hbuf_ref, cbuf_ref, hn_ref):
    tc, ch = a1_ref.shape
    first_tile = pl.program_id(1) == 0

    for r0 in range(0, tc, 64):
        hbuf_ref[CONV_HALO + r0:CONV_HALO + r0 + 64, :] = (
            a1_ref[r0:r0 + 64, :] * _sigmoid(a2_ref[r0:r0 + 64, :]))

    @pl.when(first_tile)
    def _():
        hbuf_ref[0:CONV_HALO, :] = m1_ref[...] * _sigmoid(m2_ref[...])

    @pl.when(jnp.logical_not(first_tile))
    def _():
        hbuf_ref[0:CONV_HALO, :] = p1_ref[...] * _sigmoid(p2_ref[...])

    sub = 8
    rb = 128
    for c0 in range(0, ch, LANES):
        lanes = slice(c0, c0 + LANES)
        for t0 in range(0, tc, rb):
            acc = jnp.zeros((rb, LANES), F32) + cb_ref[:, lanes]
            for s in range(sub):
                lead = 0 if s == 0 else sub
                part = None
                for a in range((CONV_WIDTH - 1 - s) // sub + 1):
                    j = CONV_WIDTH - 1 - (sub * a + s)
                    r0 = CONV_HALO + t0 - lead - sub * a
                    term = hbuf_ref[r0:r0 + rb + lead, lanes] * cw_ref[j:j + 1, lanes]
                    part = term if part is None else part + term
                acc = acc + part[lead - s:lead - s + rb, :]
            cbuf_ref[t0:t0 + rb, lanes] = acc

    rows = 32
    def ln_body(r, _):
        r0 = pl.multiple_of(r * rows, rows)
        x = cbuf_ref[pl.ds(r0, rows), :]
        mu = jnp.mean(x, axis=-1, keepdims=True)
        xc = x - mu
        var = jnp.mean(xc * xc, axis=-1, keepdims=True)
        y = xc * lax.rsqrt(var + EPS) * lg_ref[...] + lb_ref[...]
        hn_ref[pl.ds(r0, rows), :] = (y * _sigmoid(y)).astype(BF16)
        return 0
    lax.fori_loop(0, tc // rows, ln_body, 0)

    y = jnp.dot(hn_ref[...], wo_ref[...], preferred_element_type=F32)
    o_ref[...] = y * _sigmoid(gm_ref[...])


def _conv_module(proj, proj_m, conv_w, conv_b, ln_g, ln_b, w_out, nb, ls, ch, tc):
    nt = ls // tc
    hpt = tc // CONV_HALO
    meta_blk = META_TILE // CONV_HALO - 1
    merge_blk = 10240 // ch

    def cur(col):
        return lambda b, i: (b * nt + i, col)

    def prev(col):
        return lambda b, i: (jnp.maximum((b * nt + i) * hpt - 1, 0), col)

    const = lambda b, i: (0, 0)
    return pl.pallas_call(
        _conv_kernel,
        grid=(nb, nt),
        in_specs=[
            pl.BlockSpec((tc, ch), cur(0)),
            pl.BlockSpec((tc, ch), cur(1)),
            pl.BlockSpec((CONV_HALO, ch), prev(0)),
            pl.BlockSpec((CONV_HALO, ch), prev(1)),
            pl.BlockSpec((CONV_HALO, ch), lambda b, i: (meta_blk, 0)),
            pl.BlockSpec((CONV_HALO, ch), lambda b, i: (meta_blk, 1)),
            pl.BlockSpec((CONV_HALO, ch), const),
            pl.BlockSpec((1, ch), const),
            pl.BlockSpec((1, ch), const),
            pl.BlockSpec((1, ch), const),
            pl.BlockSpec((ch, ch), const),
            pl.BlockSpec((tc, ch), cur(merge_blk)),
        ],
        out_specs=pl.BlockSpec((tc, ch), cur(0)),
        out_shape=jax.ShapeDtypeStruct((nb * ls, ch), F32),
        scratch_shapes=[
            pltpu.VMEM((CONV_HALO + tc, ch), F32),
            pltpu.VMEM((tc, ch), F32),
            pltpu.VMEM((tc, ch), BF16),
        ],
        compiler_params=_params("parallel", "arbitrary"),
        name="conv_module",
    )(proj, proj, proj, proj, proj_m, proj_m, conv_w, conv_b, ln_g, ln_b, w_out, proj)


def _gla_kernel(q_ref, k_ref, v_ref, r_ref, gm_ref, a_ref, km_ref, vm_ref, am_ref, wa2_ref, ba_ref,
                ng_ref, wo_ref, o_ref, st_ref, ob_ref):
    nh = GLA_HEADS
    c = GLA_CHUNK
    tb = q_ref.shape[0]
    dk = q_ref.shape[1] // nh
    dv = v_ref.shape[1] // nh
    scale = dk ** -0.5

    row = lax.broadcasted_iota(jnp.int32, (c, c), 0)
    col = lax.broadcasted_iota(jnp.int32, (c, c), 1)
    tri = (col <= row).astype(F32)

    def log_decay(a_c):
        z = jnp.dot(a_c.astype(BF16), wa2_ref[...], preferred_element_type=F32) + ba_ref[...]
        return (jnp.minimum(z, 0.0) - jnp.log1p(jnp.exp(-jnp.abs(z)))) * (1.0 / GLA_TAU)

    def cumsum_rows(g):
        return jnp.dot(tri, g, preferred_element_type=F32, precision=lax.Precision.HIGHEST)

    @pl.when(pl.program_id(1) == 0)
    def _():
        g = log_decay(am_ref[...])
        ridx = lax.broadcasted_iota(jnp.int32, (c, 1), 0)
        g = jnp.where(ridx >= c - N_META, g, 0.0)
        b = cumsum_rows(g)
        bl = b[c - 1:c, :]
        for h in range(nh):
            ks = slice(h * dk, (h + 1) * dk)
            ke = km_ref[:, ks] * jnp.exp(bl[:, ks] - b[:, ks])
            vh = vm_ref[:, h * dv:(h + 1) * dv]
            st_ref[h] = lax.dot_general(vh.astype(BF16), ke.astype(BF16), _TN, preferred_element_type=F32)

    levels = (32, 16, 8)
    level_masks = []
    for s in levels:
        level_masks.append((row // (2 * s) == col // (2 * s)) & ((row // s) % 2 == 1) & ((col // s) % 2 == 0))
    sub = lax.broadcasted_iota(jnp.int32, (8, c), 0)
    lane = lax.broadcasted_iota(jnp.int32, (8, c), 1)

    def chunk_body(ci, carry):
        r0 = pl.multiple_of(ci * c, c)
        g_all = log_decay(a_ref[pl.ds(r0, c), :])
        b_all = cumsum_rows(g_all)
        for h in range(nh):
            ks = slice(h * dk, (h + 1) * dk)
            vs = slice(h * dv, (h + 1) * dv)
            b = b_all[:, ks]
            q = q_ref[pl.ds(r0, c), ks] * scale
            k = k_ref[pl.ds(r0, c), ks]
            vb = v_ref[pl.ds(r0, c), vs].astype(BF16)
            bl = b[c - 1:c, :]
            st = st_ref[h]

            o = lax.dot_general((q * jnp.exp(b)).astype(BF16), st.astype(BF16), _NT,
                                preferred_element_type=F32)

            att = jnp.zeros((c, c), F32)
            for s, msk in zip(levels, level_masks):
                pieces = []
                for p in range(c // (2 * s)):
                    rr = p * 2 * s + s - 1
                    pieces.append(jnp.broadcast_to(b[rr:rr + 1, :], (2 * s, dk)))
                bref = pieces[0] if len(pieces) == 1 else jnp.concatenate(pieces, axis=0)
                qs = q * jnp.exp(jnp.minimum(b - bref, 0.0))
                kk = k * jnp.exp(jnp.minimum(bref - b, 0.0))
                pm = lax.dot_general(qs.astype(BF16), kk.astype(BF16), _NT, preferred_element_type=F32)
                att = att + jnp.where(msk, pm, 0.0)

            blocks = []
            for bi in range(c // 8):
                qb = q[bi * 8:(bi + 1) * 8, :]
                bb = b[bi * 8:(bi + 1) * 8, :]
                blk = jnp.zeros((8, c), F32)
                for jj in range(8):
                    j = bi * 8 + jj
                    e = jnp.exp(jnp.minimum(bb - b[j:j + 1, :], 0.0))
                    colv = jnp.sum(qb * (k[j:j + 1, :] * e), axis=-1, keepdims=True)
                    blk = blk + jnp.where((lane == j) & (sub >= jj), colv, 0.0)
                blocks.append(blk)
            att = att + jnp.concatenate(blocks, axis=0)

            o = o + jnp.dot(att.astype(BF16), vb, preferred_element_type=F32)

            ke = k * jnp.exp(bl - b)
            st_ref[h] = st * jnp.exp(bl) + lax.dot_general(vb, ke.astype(BF16), _TN,
                                                            preferred_element_type=F32)

            ms = jnp.mean(o * o, axis=-1, keepdims=True)
            rr_ = r_ref[pl.ds(r0, c), vs]
            on = o * lax.rsqrt(ms + EPS) * ng_ref[:, vs] * (rr_ * _sigmoid(rr_))
            ob_ref[pl.ds(r0, c), vs] = on.astype(BF16)
        return carry

    lax.fori_loop(0, tb // c, chunk_body, 0)

    y = jnp.dot(ob_ref[...], wo_ref[...], preferred_element_type=F32)
    o_ref[...] = y * _sigmoid(gm_ref[...])


def _gla_module(proj, a_x, proj_m, a_m, wa2, ba, ng, w_out, nb, ls, d, tb):
    nt = ls // tb
    nqk = d // 2
    c = GLA_CHUNK
    mrow = META_TILE // c - 1

    def cur(col):
        return lambda b, i: (b * nt + i, col)

    const = lambda b, i: (0, 0)
    return pl.pallas_call(
        _gla_kernel,
        grid=(nb, nt),
        in_specs=[
            pl.BlockSpec((tb, nqk), cur(4096 // nqk)),
            pl.BlockSpec((tb, nqk), cur(4096 // nqk + 1)),
            pl.BlockSpec((tb, d), cur(6144 // d)),
            pl.BlockSpec((tb, d), cur(8192 // d)),
            pl.BlockSpec((tb, d), cur(12288 // d)),
            pl.BlockSpec((tb, LANES), cur(0)),
            pl.BlockSpec((c, nqk), lambda b, i: (mrow, 4096 // nqk + 1)),
            pl.BlockSpec((c, d), lambda b, i: (mrow, 6144 // d)),
            pl.BlockSpec((c, LANES), lambda b, i: (mrow, 0)),
            pl.BlockSpec((LANES, nqk), const),
            pl.BlockSpec((1, nqk), const),
            pl.BlockSpec((1, d), const),
            pl.BlockSpec((d, d), const),
        ],
        out_specs=pl.BlockSpec((tb, d), cur(0)),
        out_shape=jax.ShapeDtypeStruct((nb * ls, d), F32),
        scratch_shapes=[
            pltpu.VMEM((GLA_HEADS, d // GLA_HEADS, nqk // GLA_HEADS), F32),
            pltpu.VMEM((tb, d), BF16),
        ],
        compiler_params=_params("parallel", "arbitrary"),
        name="gla_module",
    )(proj, proj, proj, proj, proj, a_x, proj_m, proj_m, a_m, wa2, ba, ng, w_out)


def _mix_kernel(yc_ref, yg_ref, x_ref, wo_ref, g2_ref, wq_ref, h1_ref, xn2_ref, q2_ref):
    m = (yc_ref[...] + yg_ref[...]).astype(BF16)
    h1 = x_ref[...] + jnp.dot(m, wo_ref[...], preferred_element_type=F32)
    h1_ref[...] = h1
    ms = jnp.mean(h1 * h1, axis=-1, keepdims=True)
    xn2 = (h1 * lax.rsqrt(ms + EPS) * g2_ref[...]).astype(BF16)
    xn2_ref[...] = xn2
    q2_ref[...] = jnp.dot(xn2, wq_ref[...], preferred_element_type=F32).astype(BF16)


def _mix_module(yc, yg, x2d, w_out, g2, wq, tm):
    t, d = x2d.shape
    row = lambda i: (i, 0)
    const = lambda i: (0, 0)
    return pl.pallas_call(
        _mix_kernel,
        grid=(t // tm,),
        in_specs=[
            pl.BlockSpec((tm, d), row), pl.BlockSpec((tm, d), row), pl.BlockSpec((tm, d), row),
            pl.BlockSpec((d, d), const), pl.BlockSpec((1, d), const), pl.BlockSpec((d, d), const),
        ],
        out_specs=[pl.BlockSpec((tm, d), row), pl.BlockSpec((tm, d), row), pl.BlockSpec((tm, d), row)],
        out_shape=[jax.ShapeDtypeStruct((t, d), F32), jax.ShapeDtypeStruct((t, d), BF16),
                   jax.ShapeDtypeStruct((t, d), BF16)],
        compiler_params=_params("parallel"),
        name="mix_module",
    )(yc, yg, x2d, w_out, g2, wq)


_NO_RANK = 100.0


def _top16(s):
    idx = lax.broadcasted_iota(jnp.int32, s.shape, 0).astype(F32)
    rank = jnp.full(s.shape, _NO_RANK, F32)
    vals = []
    for r in range(PEER_TOPK):
        m = jnp.max(s, axis=0, keepdims=True)
        first = jnp.min(jnp.where(s == m, idx, 1e9), axis=0, keepdims=True)
        sel = idx == first
        rank = jnp.where(sel, float(r), rank)
        s = jnp.where(sel, -jnp.inf, s)
        vals.append(m)
    return rank, jnp.concatenate(vals, axis=0)


def _top16_distinct(s):
    rank = jnp.full(s.shape, _NO_RANK, F32)
    vals = []
    for r in range(PEER_TOPK):
        m = jnp.max(s, axis=0, keepdims=True)
        sel = s == m
        rank = jnp.where(sel, float(r), rank)
        s = jnp.where(sel, -jnp.inf, s)
        vals.append(m)
    n_ranked = jnp.sum(jnp.where(rank < float(PEER_TOPK), 1.0, 0.0), axis=0, keepdims=True)
    return rank, jnp.concatenate(vals, axis=0), n_ranked


def _frontier_counts(v1, v2):
    kk = PEER_TOPK
    r1 = lax.broadcasted_iota(jnp.int32, v1.shape, 0).astype(F32)
    cnt = jnp.zeros(v1.shape, F32)
    front = v1 + v2[0:1, :]
    for it in range(kk):
        m = jnp.max(front, axis=0, keepdims=True)
        first = jnp.min(jnp.where(front == m, r1, 99.0), axis=0, keepdims=True)
        sel = r1 == first
        cnt = cnt + sel.astype(F32)
        if it + 1 < kk:
            c_sel = jnp.sum(jnp.where(sel, cnt, 0.0), axis=0, keepdims=True)
            nxt = jnp.sum(jnp.where(r1 == c_sel, v2, 0.0), axis=0, keepdims=True)
            nxt = jnp.where(c_sel >= float(kk), -jnp.inf, nxt)
            front = jnp.where(sel, v1 + nxt, front)
    return cnt


def _peer_kernel(xn_ref, q_ref, k1_ref, k2_ref, u_ref, vt_ref, h1_ref, gf_ref, o_ref,
                 yt_ref, hta_ref, htb_ref, w_ref, p1_ref, c1_ref, r2_ref, p2_ref):
    e = pl.program_id(1)
    n_steps = pl.num_programs(1)
    tt = xn_ref.shape[0]
    te = u_ref.shape[0]
    nk = PEER_NKEYS
    kk = PEER_TOPK
    n_i1 = te // nk
    pack = 16

    def expert_acts():
        return lax.dot_general(u_ref[...], xn_ref[...], _NT, preferred_element_type=F32)

    s1_ref = htb_ref.at[0:nk]
    s2_ref = htb_ref.at[nk:2 * nk]

    @pl.when(e == 0)
    def _():
        yt_ref[...] = jnp.zeros_like(yt_ref)
        hta_ref[...] = expert_acts()

        def head_body(h, carry):
            c0 = pl.multiple_of(h * 2 * nk, 2 * nk)
            s1_ref[...] = lax.dot_general(k1_ref[h], q_ref[:, pl.ds(c0, nk)], _NT,
                                          preferred_element_type=F32)
            s2_ref[...] = lax.dot_general(k2_ref[h], q_ref[:, pl.ds(c0 + nk, nk)], _NT,
                                          preferred_element_type=F32)

            def lane_body(lt, carry2):
                l0 = pl.multiple_of(lt * LANES, LANES)
                s1 = s1_ref[:, pl.ds(l0, LANES)]
                s2 = s2_ref[:, pl.ds(l0, LANES)]

                def route(rank1, v1, rank2, v2):
                    cnt = _frontier_counts(v1, v2)
                    p1v = jnp.exp(v1 - v1[0:1, :])
                    p2v = jnp.exp(v2 - v2[0:1, :])
                    cum = jnp.zeros(v1.shape, F32)
                    for r2 in range(kk):
                        cum = cum + jnp.where(cnt > float(r2), p2v[r2:r2 + 1, :], 0.0)
                    z = jnp.sum(p1v * cum, axis=0, keepdims=True)
                    c1 = jnp.zeros(s1.shape, F32)
                    for r in range(kk):
                        c1 = jnp.where(rank1 == float(r), cnt[r:r + 1, :], c1)
                    p1_ref[h, :, pl.ds(l0, LANES)] = jnp.exp(s1 - v1[0:1, :]) / z
                    c1_ref[h, :, pl.ds(l0, LANES)] = c1
                    r2_ref[h, :, pl.ds(l0, LANES)] = rank2.astype(BF16)
                    p2_ref[h, :, pl.ds(l0, LANES)] = jnp.exp(s2 - v2[0:1, :]).astype(BF16)

                rank1, v1, n1 = _top16_distinct(s1)
                rank2, v2, n2 = _top16_distinct(s2)
                route(rank1, v1, rank2, v2)
                ties = jnp.max(jnp.abs(n1 - float(kk)) + jnp.abs(n2 - float(kk)))

                @pl.when(ties > 0.0)
                def _():
                    route(*_top16(s1), *_top16(s2))

                return carry2

            lax.fori_loop(0, tt // LANES, lane_body, 0)
            return carry

        lax.fori_loop(0, PEER_HEADS, head_body, 0)

    def main_step(ht_w, ht_r):
        ht_w[...] = expert_acts()
        i1_base = pl.multiple_of((e - 1) * n_i1, n_i1)
        for lt in range(tt // LANES):
            ls_ = slice(lt * LANES, (lt + 1) * LANES)
            c1s = [c1_ref[h, pl.ds(i1_base, n_i1), ls_] for h in range(PEER_HEADS)]
            p1s = [p1_ref[h, pl.ds(i1_base, n_i1), ls_] for h in range(PEER_HEADS)]
            for ii in range(n_i1):
                rows = slice(ii * nk, (ii + 1) * nk)
                c1b = [jnp.broadcast_to(c[ii:ii + 1, :], (pack, LANES)).astype(BF16) for c in c1s]
                p1b = [jnp.broadcast_to(p[ii:ii + 1, :], (pack, LANES)).astype(BF16) for p in p1s]
                x = ht_r[rows, ls_]
                act = (0.5 * x * (1.0 + lax.erf(x * (2.0 ** -0.5)))).astype(BF16)
                for rg in range(nk // pack):
                    rs = slice(rg * pack, (rg + 1) * pack)
                    g = jnp.zeros((pack, LANES), BF16)
                    for h in range(PEER_HEADS):
                        m = jnp.minimum(jnp.maximum(c1b[h] - r2_ref[h, rs, ls_], jnp.zeros((), BF16)), p1b[h])
                        g = g + m * p2_ref[h, rs, ls_]
                    w_ref[ii * nk + rg * pack:ii * nk + (rg + 1) * pack, ls_] = act[rs, :] * g
        yt_ref[...] += jnp.dot(vt_ref[...], w_ref[...], preferred_element_type=F32)

    odd = lax.rem(e, 2) == 1

    @pl.when(odd)
    def _():
        main_step(htb_ref, hta_ref)

    @pl.when(jnp.logical_and(e > 0, jnp.logical_not(odd)))
    def _():
        main_step(hta_ref, htb_ref)

    @pl.when(e == n_steps - 1)
    def _():
        for r0 in range(0, tt, LANES):
            h = h1_ref[r0:r0 + LANES, :] + yt_ref[:, r0:r0 + LANES].T
            ms = jnp.mean(h * h, axis=-1, keepdims=True)
            o_ref[r0:r0 + LANES, :] = h * lax.rsqrt(ms + EPS) * gf_ref[...]


def _peer_module(xn2, q2, k1, k2, u, vt, h1, gf, tt, te):
    t, d = xn2.shape
    ne = u.shape[0] // te
    nk = PEER_NKEYS
    sel32 = pltpu.VMEM((PEER_HEADS, nk, tt), F32)
    sel16 = pltpu.VMEM((PEER_HEADS, nk, tt), BF16)
    return pl.pallas_call(
        _peer_kernel,
        grid=(t // tt, ne + 1),
        in_specs=[
            pl.BlockSpec((tt, d), lambda i, e: (i, 0)),
            pl.BlockSpec((tt, d), lambda i, e: (i, 0), pipeline_mode=pl.Buffered(1)),
            pl.BlockSpec((PEER_HEADS, nk, nk), lambda i, e: (0, 0, 0)),
            pl.BlockSpec((PEER_HEADS, nk, nk), lambda i, e: (0, 0, 0)),
            pl.BlockSpec((te, d), lambda i, e: (jnp.minimum(e, ne - 1), 0)),
            pl.BlockSpec((d, te), lambda i, e: (0, jnp.maximum(e - 1, 0))),
            pl.BlockSpec((tt, d), lambda i, e: (i, 0)),
            pl.BlockSpec((1, d), lambda i, e: (0, 0)),
        ],
        out_specs=pl.BlockSpec((tt, d), lambda i, e: (i, 0)),
        out_shape=jax.ShapeDtypeStruct((t, d), F32),
        scratch_shapes=[
            pltpu.VMEM((d, tt), F32),
            pltpu.VMEM((te, tt), F32), pltpu.VMEM((te, tt), F32),
            pltpu.VMEM((te, tt), BF16),
            sel32, sel32, sel16, sel16,
        ],
        compiler_params=_params("parallel", "arbitrary"),
        name="peer_module",
    )(xn2, q2, k1, k2, u, vt, h1, gf)


def _layer(x2d, xm, nb, ls, norm1_g, w_in, w_alpha2, b_alpha, conv_w, conv_b, conv_ln_g, conv_ln_b,
           w_conv_out, gla_norm_g, w_gla_out, w_out, norm2_g, peer_wq, peer_k1, peer_k2, peer_u, peer_v, normf_g):
    d = x2d.shape[1]
    n_pre = 5 * d
    rank = w_alpha2.shape[0]
    w_main = jnp.concatenate([w_in[:, :n_pre], w_in[:, n_pre + rank:]], axis=1).astype(BF16)
    w_a = jnp.pad(w_in[:, n_pre:n_pre + rank], ((0, 0), (0, LANES - rank))).astype(BF16)
    g1 = norm1_g.reshape(1, d)

    tm = min(1024, x2d.shape[0])
    proj, a_x = _in_proj(x2d, g1, w_main, w_a, tm, INPROJ_TN)
    proj_m, a_m = _in_proj(xm, g1, w_main, w_a, META_TILE, INPROJ_TN)

    cw = jnp.pad(conv_w, ((0, CONV_HALO - CONV_WIDTH), (0, 0)))
    yc = _conv_module(proj, proj_m, cw, conv_b.reshape(1, d), conv_ln_g.reshape(1, d), conv_ln_b.reshape(1, d),
                      w_conv_out.astype(BF16), nb, ls, d, 256)

    wa2 = jnp.pad(w_alpha2, ((0, LANES - rank), (0, 0))).astype(BF16)
    yg = _gla_module(proj, a_x, proj_m, a_m, wa2, b_alpha.reshape(1, -1), gla_norm_g.reshape(1, d),
                     w_gla_out.astype(BF16), nb, ls, d, 256)

    h1, xn2, q2 = _mix_module(yc, yg, x2d, w_out.astype(BF16), norm2_g.reshape(1, d), peer_wq.astype(BF16), 256)

    return _peer_module(xn2, q2, peer_k1.astype(BF16), peer_k2.astype(BF16), peer_u.astype(BF16),
                        peer_v.T.astype(BF16), h1, normf_g.reshape(1, d), min(512, x2d.shape[0]), PEER_TE)


def kernel(x, meta_tokens, norm1_g, w_in, w_alpha2, b_alpha, conv_w, conv_b, conv_ln_g, conv_ln_b, w_conv_out,
           gla_norm_g, w_gla_out, w_out, norm2_g, peer_wq, peer_k1, peer_k2, peer_u, peer_v, normf_g):
    nb, ls, d = x.shape
    depth = w_in.shape[0]
    assert depth == 1, "meta-token positions are only carried through one layer"
    x2d = x.reshape(nb * ls, d)
    xm = jnp.concatenate([jnp.zeros((META_TILE - N_META, d), x.dtype), meta_tokens.astype(x.dtype)], axis=0)
    out = _layer(x2d, xm, nb, ls, norm1_g[0], w_in[0], w_alpha2[0], b_alpha[0], conv_w[0], conv_b[0],
                 conv_ln_g[0], conv_ln_b[0], w_conv_out[0], gla_norm_g[0], w_gla_out[0], w_out[0],
                 norm2_g[0], peer_wq[0], peer_k1[0], peer_k2[0], peer_u[0], peer_v[0], normf_g)
    return out.reshape(nb, ls, d)
```

```python
import functools

import jax
import jax.numpy as jnp
from jax import lax
from jax.experimental import pallas as pl
from jax.experimental.pallas import tpu as pltpu

F32 = jnp.float32
BF16 = jnp.bfloat16

N_META = 16
CONV_WIDTH = 31
GLA_HEADS = 4
GLA_TAU = 16.0
GLA_CHUNK = 64
PEER_HEADS = 8
PEER_NKEYS = 128
PEER_TOPK = 16
EPS = 1e-6

LANES = 128
META_TILE = 256
CONV_HALO = 32
INPROJ_TN = 1024
SEQ_TILE = 512
PEER_TE = 1024
VMEM_LIMIT_BYTES = 60 * 1024 * 1024

_NT = (((1,), (1,)), ((), ()))
_TN = (((0,), (0,)), ((), ()))


def _sigmoid(x):
    return 1.0 / (1.0 + jnp.exp(-x))


def _params(*sem):
    return pltpu.CompilerParams(dimension_semantics=sem, vmem_limit_bytes=VMEM_LIMIT_BYTES)


def _inproj_kernel(x_ref, g_ref, w_ref, wa_ref, o_ref, a_ref, xn_ref):
    @pl.when(pl.program_id(1) == 0)
    def _():
        rows = 128
        def body(r, _):
            r0 = pl.multiple_of(r * rows, rows)
            x = x_ref[pl.ds(r0, rows), :]
            ms = jnp.mean(x * x, axis=-1, keepdims=True)
            xn_ref[pl.ds(r0, rows), :] = (x * lax.rsqrt(ms + EPS) * g_ref[...]).astype(BF16)
            return 0
        lax.fori_loop(0, x_ref.shape[0] // rows, body, 0)
        a_ref[...] = jnp.dot(xn_ref[...], wa_ref[...], preferred_element_type=F32)

    o_ref[...] = jnp.dot(xn_ref[...], w_ref[...], preferred_element_type=F32)


def _in_proj(x2d, g, w_main, w_a, tm, tn):
    t, d = x2d.shape
    n = w_main.shape[1]
    return pl.pallas_call(
        _inproj_kernel,
        grid=(t // tm, n // tn),
        in_specs=[
            pl.BlockSpec((tm, d), lambda i, j: (i, 0)),
            pl.BlockSpec((1, d), lambda i, j: (0, 0)),
            pl.BlockSpec((d, tn), lambda i, j: (0, j)),
            pl.BlockSpec((d, LANES), lambda i, j: (0, 0)),
        ],
        out_specs=[
            pl.BlockSpec((tm, tn), lambda i, j: (i, j)),
            pl.BlockSpec((tm, LANES), lambda i, j: (i, 0)),
        ],
        out_shape=[jax.ShapeDtypeStruct((t, n), F32), jax.ShapeDtypeStruct((t, LANES), F32)],
        scratch_shapes=[pltpu.VMEM((tm, d), BF16)],
        compiler_params=_params("parallel", "arbitrary"),
        name="in_proj",
    )(x2d, g, w_main, w_a)


def _conv_kernel(a1_ref, a2_ref, p1_ref, p2_ref, m1_ref, m2_ref, cw_ref, cb_ref, lg_ref, lb_ref,
                 wo_ref, gm_ref, o_ref, hbuf_ref, cbuf_ref, hn_ref):
    tc, ch = a1_ref.shape
    first_tile = pl.program_id(1) == 0

    for r0 in range(0, tc, 64):
        hbuf_ref[CONV_HALO + r0:CONV_HALO + r0 + 64, :] = (
            a1_ref[r0:r0 + 64, :] * _sigmoid(a2_ref[r0:r0 + 64, :]))

    @pl.when(first_tile)
    def _():
        hbuf_ref[0:CONV_HALO, :] = m1_ref[...] * _sigmoid(m2_ref[...])

    @pl.when(jnp.logical_not(first_tile))
    def _():
        hbuf_ref[0:CONV_HALO, :] = p1_ref[...] * _sigmoid(p2_ref[...])

    sub = 8
    rb = 128
    for c0 in range(0, ch, LANES):
        lanes = slice(c0, c0 + LANES)
        for t0 in range(0, tc, rb):
            acc = jnp.zeros((rb, LANES), F32) + cb_ref[:, lanes]
            for s in range(sub):
                lead = 0 if s == 0 else sub
                part = None
                for a in range((CONV_WIDTH - 1 - s) // sub + 1):
                    j = CONV_WIDTH - 1 - (sub * a + s)
                    r0 = CONV_HALO + t0 - lead - sub * a
                    term = hbuf_ref[r0:r0 + rb + lead, lanes] * cw_ref[j:j + 1, lanes]
                    part = term if part is None else part + term
                acc = acc + part[lead - s:lead - s + rb, :]
            cbuf_ref[t0:t0 + rb, lanes] = acc

    rows = 32
    def ln_body(r, _):
        r0 = pl.multiple_of(r * rows, rows)
        x = cbuf_ref[pl.ds(r0, rows), :]
        mu = jnp.mean(x, axis=-1, keepdims=True)
        xc = x - mu
        var = jnp.mean(xc * xc, axis=-1, keepdims=True)
        y = xc * lax.rsqrt(var + EPS) * lg_ref[...] + lb_ref[...]
        hn_ref[pl.ds(r0, rows), :] = (y * _sigmoid(y)).astype(BF16)
        return 0
    lax.fori_loop(0, tc // rows, ln_body, 0)

    y = jnp.dot(hn_ref[...], wo_ref[...], preferred_element_type=F32)
    o_ref[...] = y * _sigmoid(gm_ref[...])


def _conv_module(proj, proj_m, conv_w, conv_b, ln_g, ln_b, w_out, nb, ls, ch, tc):
    nt = ls // tc
    hpt = tc // CONV_HALO
    meta_blk = META_TILE // CONV_HALO - 1
    merge_blk = 10240 // ch

    def cur(col):
        return lambda b, i: (b * nt + i, col)

    def prev(col):
        return lambda b, i: (jnp.maximum((b * nt + i) * hpt - 1, 0), col)

    const = lambda b, i: (0, 0)
    return pl.pallas_call(
        _conv_kernel,
        grid=(nb, nt),
        in_specs=[
            pl.BlockSpec((tc, ch), cur(0)),
            pl.BlockSpec((tc, ch), cur(1)),
            pl.BlockSpec((CONV_HALO, ch), prev(0)),
            pl.BlockSpec((CONV_HALO, ch), prev(1)),
            pl.BlockSpec((CONV_HALO, ch), lambda b, i: (meta_blk, 0)),
            pl.BlockSpec((CONV_HALO, ch), lambda b, i: (meta_blk, 1)),
            pl.BlockSpec((CONV_HALO, ch), const),
            pl.BlockSpec((1, ch), const),
            pl.BlockSpec((1, ch), const),
            pl.BlockSpec((1, ch), const),
            pl.BlockSpec((ch, ch), const, pipeline_mode=pl.Buffered(1)),
            pl.BlockSpec((tc, ch), cur(merge_blk)),
        ],
        out_specs=pl.BlockSpec((tc, ch), cur(0)),
        out_shape=jax.ShapeDtypeStruct((nb * ls, ch), F32),
        scratch_shapes=[
            pltpu.VMEM((CONV_HALO + tc, ch), F32),
            pltpu.VMEM((tc, ch), F32),
            pltpu.VMEM((tc, ch), BF16),
        ],
        compiler_params=_params("parallel", "arbitrary"),
        name="conv_module",
    )(proj, proj, proj, proj, proj_m, proj_m, conv_w, conv_b, ln_g, ln_b, w_out, proj)


def _gla_kernel(q_ref, k_ref, v_ref, r_ref, gm_ref, a_ref, km_ref, vm_ref, am_ref, wa2_ref, ba_ref,
                ng_ref, wo_ref, o_ref, st_ref, ob_ref):
    nh = GLA_HEADS
    c = GLA_CHUNK
    tb = q_ref.shape[0]
    dk = q_ref.shape[1] // nh
    dv = v_ref.shape[1] // nh
    scale = dk ** -0.5

    row = lax.broadcasted_iota(jnp.int32, (c, c), 0)
    col = lax.broadcasted_iota(jnp.int32, (c, c), 1)
    tri = (col <= row).astype(F32)

    def log_decay(a_c):
        z = jnp.dot(a_c.astype(BF16), wa2_ref[...], preferred_element_type=F32) + ba_ref[...]
        return (jnp.minimum(z, 0.0) - jnp.log1p(jnp.exp(-jnp.abs(z)))) * (1.0 / GLA_TAU)

    def cumsum_rows(g):
        return jnp.dot(tri, g, preferred_element_type=F32, precision=lax.Precision.HIGHEST)

    @pl.when(pl.program_id(1) == 0)
    def _():
        g = log_decay(am_ref[...])
        ridx = lax.broadcasted_iota(jnp.int32, (c, 1), 0)
        g = jnp.where(ridx >= c - N_META, g, 0.0)
        b = cumsum_rows(g)
        bl = b[c - 1:c, :]
        for h in range(nh):
            ks = slice(h * dk, (h + 1) * dk)
            ke = km_ref[:, ks] * jnp.exp(bl[:, ks] - b[:, ks])
            vh = vm_ref[:, h * dv:(h + 1) * dv]
            st_ref[h] = lax.dot_general(vh.astype(BF16), ke.astype(BF16), _TN, preferred_element_type=F32)

    levels = (32, 16, 8)
    level_masks = []
    for s in levels:
        level_masks.append((row // (2 * s) == col // (2 * s)) & ((row // s) % 2 == 1) & ((col // s) % 2 == 0))
    sub = lax.broadcasted_iota(jnp.int32, (8, c), 0)
    lane = lax.broadcasted_iota(jnp.int32, (8, c), 1)

    def chunk_body(ci, carry):
        r0 = pl.multiple_of(ci * c, c)
        g_all = log_decay(a_ref[pl.ds(r0, c), :])
        b_all = cumsum_rows(g_all)
        for h in range(nh):
            ks = slice(h * dk, (h + 1) * dk)
            vs = slice(h * dv, (h + 1) * dv)
            b = b_all[:, ks]
            q = q_ref[pl.ds(r0, c), ks] * scale
            k = k_ref[pl.ds(r0, c), ks]
            vb = v_ref[pl.ds(r0, c), vs].astype(BF16)
            bl = b[c - 1:c, :]
            st = st_ref[h]

            o = lax.dot_general((q * jnp.exp(b)).astype(BF16), st.astype(BF16), _NT,
                                preferred_element_type=F32)

            att = jnp.zeros((c, c), F32)
            for s, msk in zip(levels, level_masks):
                pieces = []
                for p in range(c // (2 * s)):
                    rr = p * 2 * s + s - 1
                    pieces.append(jnp.broadcast_to(b[rr:rr + 1, :], (2 * s, dk)))
                bref = pieces[0] if len(pieces) == 1 else jnp.concatenate(pieces, axis=0)
                qs = q * jnp.exp(jnp.minimum(b - bref, 0.0))
                kk = k * jnp.exp(jnp.minimum(bref - b, 0.0))
                pm = lax.dot_general(qs.astype(BF16), kk.astype(BF16), _NT, preferred_element_type=F32)
                att = att + jnp.where(msk, pm, 0.0)

            blocks = []
            for bi in range(c // 8):
                qb = q[bi * 8:(bi + 1) * 8, :]
                bb = b[bi * 8:(bi + 1) * 8, :]
                blk = jnp.zeros((8, c), F32)
                for jj in range(8):
                    j = bi * 8 + jj
                    e = jnp.exp(jnp.minimum(bb - b[j:j + 1, :], 0.0))
                    colv = jnp.sum(qb * (k[j:j + 1, :] * e), axis=-1, keepdims=True)
                    blk = blk + jnp.where((lane == j) & (sub >= jj), colv, 0.0)
                blocks.append(blk)
            att = att + jnp.concatenate(blocks, axis=0)

            o = o + jnp.dot(att.astype(BF16), vb, preferred_element_type=F32)

            ke = k * jnp.exp(bl - b)
            st_ref[h] = st * jnp.exp(bl) + lax.dot_general(vb, ke.astype(BF16), _TN,
                                                            preferred_element_type=F32)

            ms = jnp.mean(o * o, axis=-1, keepdims=True)
            rr_ = r_ref[pl.ds(r0, c), vs]
            on = o * lax.rsqrt(ms + EPS) * ng_ref[:, vs] * (rr_ * _sigmoid(rr_))
            ob_ref[pl.ds(r0, c), vs] = on.astype(BF16)
        return carry

    lax.fori_loop(0, tb // c, chunk_body, 0)

    y = jnp.dot(ob_ref[...], wo_ref[...], preferred_element_type=F32)
    o_ref[...] = y * _sigmoid(gm_ref[...])


def _gla_module(proj, a_x, proj_m, a_m, wa2, ba, ng, w_out, nb, ls, d, tb):
    nt = ls // tb
    nqk = d // 2
    c = GLA_CHUNK
    mrow = META_TILE // c - 1

    def cur(col):
        return lambda b, i: (b * nt + i, col)

    const = lambda b, i: (0, 0)
    return pl.pallas_call(
        _gla_kernel,
        grid=(nb, nt),
        in_specs=[
            pl.BlockSpec((tb, nqk), cur(4096 // nqk)),
            pl.BlockSpec((tb, nqk), cur(4096 // nqk + 1)),
            pl.BlockSpec((tb, d), cur(6144 // d)),
            pl.BlockSpec((tb, d), cur(8192 // d)),
            pl.BlockSpec((tb, d), cur(12288 // d)),
            pl.BlockSpec((tb, LANES), cur(0)),
            pl.BlockSpec((c, nqk), lambda b, i: (mrow, 4096 // nqk + 1)),
            pl.BlockSpec((c, d), lambda b, i: (mrow, 6144 // d)),
            pl.BlockSpec((c, LANES), lambda b, i: (mrow, 0)),
            pl.BlockSpec((LANES, nqk), const),
            pl.BlockSpec((1, nqk), const),
            pl.BlockSpec((1, d), const),
            pl.BlockSpec((d, d), const, pipeline_mode=pl.Buffered(1)),
        ],
        out_specs=pl.BlockSpec((tb, d), cur(0)),
        out_shape=jax.ShapeDtypeStruct((nb * ls, d), F32),
        scratch_shapes=[
            pltpu.VMEM((GLA_HEADS, d // GLA_HEADS, nqk // GLA_HEADS), F32),
            pltpu.VMEM((tb, d), BF16),
        ],
        compiler_params=_params("parallel", "arbitrary"),
        name="gla_module",
    )(proj, proj, proj, proj, proj, a_x, proj_m, proj_m, a_m, wa2, ba, ng, w_out)


def _mix_kernel(yc_ref, yg_ref, x_ref, wo_ref, g2_ref, wq_ref, h1_ref, xn2_ref, q2_ref):
    m = (yc_ref[...] + yg_ref[...]).astype(BF16)
    h1 = x_ref[...] + jnp.dot(m, wo_ref[...], preferred_element_type=F32)
    h1_ref[...] = h1
    ms = jnp.mean(h1 * h1, axis=-1, keepdims=True)
    xn2 = (h1 * lax.rsqrt(ms + EPS) * g2_ref[...]).astype(BF16)
    xn2_ref[...] = xn2
    q2_ref[...] = jnp.dot(xn2, wq_ref[...], preferred_element_type=F32).astype(BF16)


def _mix_module(yc, yg, x2d, w_out, g2, wq, tm):
    t, d = x2d.shape
    row = lambda i: (i, 0)
    const = lambda i: (0, 0)
    return pl.pallas_call(
        _mix_kernel,
        grid=(t // tm,),
        in_specs=[
            pl.BlockSpec((tm, d), row), pl.BlockSpec((tm, d), row), pl.BlockSpec((tm, d), row),
            pl.BlockSpec((d, d), const), pl.BlockSpec((1, d), const), pl.BlockSpec((d, d), const),
        ],
        out_specs=[pl.BlockSpec((tm, d), row), pl.BlockSpec((tm, d), row), pl.BlockSpec((tm, d), row)],
        out_shape=[jax.ShapeDtypeStruct((t, d), F32), jax.ShapeDtypeStruct((t, d), BF16),
                   jax.ShapeDtypeStruct((t, d), BF16)],
        compiler_params=_params("parallel"),
        name="mix_module",
    )(yc, yg, x2d, w_out, g2, wq)


_NO_RANK = 100.0


def _top16(s):
    idx = lax.broadcasted_iota(jnp.int32, s.shape, 0).astype(F32)
    rank = jnp.full(s.shape, _NO_RANK, F32)
    vals = []
    for r in range(PEER_TOPK):
        m = jnp.max(s, axis=0, keepdims=True)
        first = jnp.min(jnp.where(s == m, idx, 1e9), axis=0, keepdims=True)
        sel = idx == first
        rank = jnp.where(sel, float(r), rank)
        s = jnp.where(sel, -jnp.inf, s)
        vals.append(m)
    return rank, jnp.concatenate(vals, axis=0)


def _top16_distinct(s):
    rank = jnp.full(s.shape, _NO_RANK, F32)
    vals = []
    for r in range(PEER_TOPK):
        m = jnp.max(s, axis=0, keepdims=True)
        sel = s == m
        rank = jnp.where(sel, float(r), rank)
        s = jnp.where(sel, -jnp.inf, s)
        vals.append(m)
    n_ranked = jnp.sum(jnp.where(rank < float(PEER_TOPK), 1.0, 0.0), axis=0, keepdims=True)
    return rank, jnp.concatenate(vals, axis=0), n_ranked


def _frontier_counts(v1, v2):
    kk = PEER_TOPK
    r1 = lax.broadcasted_iota(jnp.int32, v1.shape, 0).astype(F32)
    cnt = jnp.zeros(v1.shape, F32)
    front = v1 + v2[0:1, :]
    for it in range(kk):
        m = jnp.max(front, axis=0, keepdims=True)
        first = jnp.min(jnp.where(front == m, r1, 99.0), axis=0, keepdims=True)
        sel = r1 == first
        cnt = cnt + sel.astype(F32)
        if it + 1 < kk:
            c_sel = jnp.sum(jnp.where(sel, cnt, 0.0), axis=0, keepdims=True)
            nxt = jnp.sum(jnp.where(r1 == c_sel, v2, 0.0), axis=0, keepdims=True)
            nxt = jnp.where(c_sel >= float(kk), -jnp.inf, nxt)
            front = jnp.where(sel, v1 + nxt, front)
    return cnt


def _peer_kernel(xn_ref, q_ref, k1_ref, k2_ref, u_ref, vt_ref, h1_ref, gf_ref, o_ref,
                 yt_ref, hta_ref, htb_ref, w_ref, p1_ref, c1_ref, r2_ref, p2_ref):
    e = pl.program_id(1)
    n_steps = pl.num_programs(1)
    tt = xn_ref.shape[0]
    te = u_ref.shape[0]
    nk = PEER_NKEYS
    kk = PEER_TOPK
    n_i1 = te // nk
    pack = 16

    def expert_acts():
        return lax.dot_general(u_ref[...], xn_ref[...], _NT, preferred_element_type=F32)

    s1_ref = htb_ref.at[0:nk]
    s2_ref = htb_ref.at[nk:2 * nk]

    @pl.when(e == 0)
    def _():
        yt_ref[...] = jnp.zeros_like(yt_ref)
        hta_ref[...] = expert_acts()

        def head_body(h, carry):
            c0 = pl.multiple_of(h * 2 * nk, 2 * nk)
            s1_ref[...] = lax.dot_general(k1_ref[h], q_ref[:, pl.ds(c0, nk)], _NT,
                                          preferred_element_type=F32)
            s2_ref[...] = lax.dot_general(k2_ref[h], q_ref[:, pl.ds(c0 + nk, nk)], _NT,
                                          preferred_element_type=F32)

            def lane_body(lt, carry2):
                l0 = pl.multiple_of(lt * LANES, LANES)
                s1 = s1_ref[:, pl.ds(l0, LANES)]
                s2 = s2_ref[:, pl.ds(l0, LANES)]

                def route(rank1, v1, rank2, v2):
                    cnt = _frontier_counts(v1, v2)
                    p1v = jnp.exp(v1 - v1[0:1, :])
                    p2v = jnp.exp(v2 - v2[0:1, :])
                    cum = jnp.zeros(v1.shape, F32)
                    for r2 in range(kk):
                        cum = cum + jnp.where(cnt > float(r2), p2v[r2:r2 + 1, :], 0.0)
                    z = jnp.sum(p1v * cum, axis=0, keepdims=True)
                    c1 = jnp.zeros(s1.shape, F32)
                    for r in range(kk):
                        c1 = jnp.where(rank1 == float(r), cnt[r:r + 1, :], c1)
                    p1_ref[h, :, pl.ds(l0, LANES)] = jnp.exp(s1 - v1[0:1, :]) / z
                    c1_ref[h, :, pl.ds(l0, LANES)] = c1
                    r2_ref[h, :, pl.ds(l0, LANES)] = rank2.astype(BF16)
                    p2_ref[h, :, pl.ds(l0, LANES)] = jnp.exp(s2 - v2[0:1, :]).astype(BF16)

                rank1, v1, n1 = _top16_distinct(s1)
                rank2, v2, n2 = _top16_distinct(s2)
                route(rank1, v1, rank2, v2)
                ties = jnp.max(jnp.abs(n1 - float(kk)) + jnp.abs(n2 - float(kk)))

                @pl.when(ties > 0.0)
                def _():
                    route(*_top16(s1), *_top16(s2))

                return carry2

            lax.fori_loop(0, tt // LANES, lane_body, 0)
            return carry

        lax.fori_loop(0, PEER_HEADS, head_body, 0)

    def main_step(ht_w, ht_r):
        ht_w[...] = expert_acts()
        i1_base = pl.multiple_of((e - 1) * n_i1, n_i1)
        for lt in range(tt // LANES):
            ls_ = slice(lt * LANES, (lt + 1) * LANES)
            c1s = [c1_ref[h, pl.ds(i1_base, n_i1), ls_] for h in range(PEER_HEADS)]
            p1s = [p1_ref[h, pl.ds(i1_base, n_i1), ls_] for h in range(PEER_HEADS)]
            for ii in range(n_i1):
                rows = slice(ii * nk, (ii + 1) * nk)
                c1b = [jnp.broadcast_to(c[ii:ii + 1, :], (pack, LANES)).astype(BF16) for c in c1s]
                p1b = [jnp.broadcast_to(p[ii:ii + 1, :], (pack, LANES)).astype(BF16) for p in p1s]
                x = ht_r[rows, ls_]
                act = (0.5 * x * (1.0 + lax.erf(x * (2.0 ** -0.5)))).astype(BF16)
                for rg in range(nk // pack):
                    rs = slice(rg * pack, (rg + 1) * pack)
                    g = jnp.zeros((pack, LANES), BF16)
                    for h in range(PEER_HEADS):
                        m = jnp.minimum(jnp.maximum(c1b[h] - r2_ref[h, rs, ls_], jnp.zeros((), BF16)), p1b[h])
                        g = g + m * p2_ref[h, rs, ls_]
                    w_ref[ii * nk + rg * pack:ii * nk + (rg + 1) * pack, ls_] = act[rs, :] * g
        yt_ref[...] += jnp.dot(vt_ref[...], w_ref[...], preferred_element_type=F32)

    odd = lax.rem(e, 2) == 1

    @pl.when(odd)
    def _():
        main_step(htb_ref, hta_ref)

    @pl.when(jnp.logical_and(e > 0, jnp.logical_not(odd)))
    def _():
        main_step(hta_ref, htb_ref)

    @pl.when(e == n_steps - 1)
    def _():
        for r0 in range(0, tt, LANES):
            h = h1_ref[r0:r0 + LANES, :] + yt_ref[:, r0:r0 + LANES].T
            ms = jnp.mean(h * h, axis=-1, keepdims=True)
            o_ref[r0:r0 + LANES, :] = h * lax.rsqrt(ms + EPS) * gf_ref[...]


def _peer_module(xn2, q2, k1, k2, u, vt, h1, gf, tt, te):
    t, d = xn2.shape
    ne = u.shape[0] // te
    nk = PEER_NKEYS
    sel32 = pltpu.VMEM((PEER_HEADS, nk, tt), F32)
    sel16 = pltpu.VMEM((PEER_HEADS, nk, tt), BF16)
    return pl.pallas_call(
        _peer_kernel,
        grid=(t // tt, ne + 1),
        in_specs=[
            pl.BlockSpec((tt, d), lambda i, e: (i, 0)),
            pl.BlockSpec((tt, d), lambda i, e: (i, 0), pipeline_mode=pl.Buffered(1)),
            pl.BlockSpec((PEER_HEADS, nk, nk), lambda i, e: (0, 0, 0)),
            pl.BlockSpec((PEER_HEADS, nk, nk), lambda i, e: (0, 0, 0)),
            pl.BlockSpec((te, d), lambda i, e: (jnp.minimum(e, ne - 1), 0)),
            pl.BlockSpec((d, te), lambda i, e: (0, jnp.maximum(e - 1, 0))),
            pl.BlockSpec((tt, d), lambda i, e: (i, 0)),
            pl.BlockSpec((1, d), lambda i, e: (0, 0)),
        ],
        out_specs=pl.BlockSpec((tt, d), lambda i, e: (i, 0)),
        out_shape=jax.ShapeDtypeStruct((t, d), F32),
        scratch_shapes=[
            pltpu.VMEM((d, tt), F32),
            pltpu.VMEM((te, tt), F32), pltpu.VMEM((te, tt), F32),
            pltpu.VMEM((te, tt), BF16),
            sel32, sel32, sel16, sel16,
        ],
        compiler_params=_params("parallel", "arbitrary"),
        name="peer_module",
    )(xn2, q2, k1, k2, u, vt, h1, gf)


def _layer(x2d, xm, nb, ls, norm1_g, w_in, w_alpha2, b_alpha, conv_w, conv_b, conv_ln_g, conv_ln_b,
           w_conv_out, gla_norm_g, w_gla_out, w_out, norm2_g, peer_wq, peer_k1, peer_k2, peer_u, peer_v, normf_g):
    d = x2d.shape[1]
    n_pre = 5 * d
    rank = w_alpha2.shape[0]
    w_bf = w_in.astype(BF16)
    w_main = jnp.concatenate([w_bf[:, :n_pre], w_bf[:, n_pre + rank:]], axis=1)
    w_a = jnp.pad(w_bf[:, n_pre:n_pre + rank], ((0, 0), (0, LANES - rank)))
    g1 = norm1_g.reshape(1, d)

    tm = min(1024, x2d.shape[0])
    proj, a_x = _in_proj(x2d, g1, w_main, w_a, tm, INPROJ_TN)
    proj_m, a_m = _in_proj(xm, g1, w_main, w_a, META_TILE, INPROJ_TN)

    cw = jnp.pad(conv_w, ((0, CONV_HALO - CONV_WIDTH), (0, 0)))
    yc = _conv_module(proj, proj_m, cw, conv_b.reshape(1, d), conv_ln_g.reshape(1, d), conv_ln_b.reshape(1, d),
                      w_conv_out.astype(BF16), nb, ls, d, min(SEQ_TILE, ls))

    wa2 = jnp.pad(w_alpha2, ((0, LANES - rank), (0, 0))).astype(BF16)
    yg = _gla_module(proj, a_x, proj_m, a_m, wa2, b_alpha.reshape(1, -1), gla_norm_g.reshape(1, d),
                     w_gla_out.astype(BF16), nb, ls, d, min(SEQ_TILE, ls))

    h1, xn2, q2 = _mix_module(yc, yg, x2d, w_out.astype(BF16), norm2_g.reshape(1, d), peer_wq.astype(BF16), 256)

    return _peer_module(xn2, q2, peer_k1.astype(BF16), peer_k2.astype(BF16), peer_u.astype(BF16),
                        peer_v.astype(BF16).T, h1, normf_g.reshape(1, d), min(512, x2d.shape[0]), PEER_TE)


def kernel(x, meta_tokens, norm1_g, w_in, w_alpha2, b_alpha, conv_w, conv_b, conv_ln_g, conv_ln_b, w_conv_out,
           gla_norm_g, w_gla_out, w_out, norm2_g, peer_wq, peer_k1, peer_k2, peer_u, peer_v, normf_g):
    nb, ls, d = x.shape
    depth = w_in.shape[0]
    assert depth == 1, "meta-token positions are only carried through one layer"
    x2d = x.reshape(nb * ls, d)
    xm = jnp.concatenate([jnp.zeros((META_TILE - N_META, d), x.dtype), meta_tokens.astype(x.dtype)], axis=0)
    out = _layer(x2d, xm, nb, ls, norm1_g[0], w_in[0], w_alpha2[0], b_alpha[0], conv_w[0], conv_b[0],
                 conv_ln_g[0], conv_ln_b[0], w_conv_out[0], gla_norm_g[0], w_gla_out[0], w_out[0],
                 norm2_g[0], peer_wq[0], peer_k1[0], peer_k2[0], peer_u[0], peer_v[0], normf_g)
    return out.reshape(nb, ls, d)
```

```python
import jax
import jax.numpy as jnp
from jax import lax
from jax.experimental import pallas as pl
from jax.experimental.pallas import tpu as pltpu

F32 = jnp.float32
BF16 = jnp.bfloat16

N_META = 16
CONV_WIDTH = 31
GLA_HEADS = 4
GLA_TAU = 16.0
GLA_CHUNK = 64
PEER_HEADS = 8
PEER_NKEYS = 128
PEER_TOPK = 16
EPS = 1e-6

LANES = 128
META_TILE = 256
CONV_HALO = 32
INPROJ_TM = 1024
INPROJ_TN = 1024
MIX_TM = 256
PEER_TT = 512
SEQ_TILE = 512
PEER_TE = 1024
VMEM_LIMIT_BYTES = 60 * 1024 * 1024

_NT = (((1,), (1,)), ((), ()))
_TN = (((0,), (0,)), ((), ()))


def _sigmoid(x):
    return 1.0 / (1.0 + jnp.exp(-x))


def _proj_col_starts(d):
    widths = (("conv", 2 * d), ("q", d // 2), ("k", d // 2), ("v", d), ("r", d), ("gate_conv", d), ("gate_gla", d))
    starts, at = {}, 0
    for name, width in widths:
        starts[name] = at
        at += width
    return starts


def _params(*sem):
    return pltpu.CompilerParams(dimension_semantics=sem, vmem_limit_bytes=VMEM_LIMIT_BYTES)


def _inproj_kernel(x_ref, g_ref, w_ref, wa_ref, o_ref, a_ref, xn_ref):
    @pl.when(pl.program_id(1) == 0)
    def _():
        rows = 128
        def body(r, _):
            r0 = pl.multiple_of(r * rows, rows)
            x = x_ref[pl.ds(r0, rows), :]
            ms = jnp.mean(x * x, axis=-1, keepdims=True)
            xn_ref[pl.ds(r0, rows), :] = (x * lax.rsqrt(ms + EPS) * g_ref[...]).astype(BF16)
            return 0
        lax.fori_loop(0, x_ref.shape[0] // rows, body, 0)
        a_ref[...] = jnp.dot(xn_ref[...], wa_ref[...], preferred_element_type=F32)

    o_ref[...] = jnp.dot(xn_ref[...], w_ref[...], preferred_element_type=F32)


def _in_proj(x2d, g, w_main, w_a, tm, tn):
    t, d = x2d.shape
    n = w_main.shape[1]
    return pl.pallas_call(
        _inproj_kernel,
        grid=(t // tm, n // tn),
        in_specs=[
            pl.BlockSpec((tm, d), lambda i, j: (i, 0)),
            pl.BlockSpec((1, d), lambda i, j: (0, 0)),
            pl.BlockSpec((d, tn), lambda i, j: (0, j)),
            pl.BlockSpec((d, LANES), lambda i, j: (0, 0)),
        ],
        out_specs=[
            pl.BlockSpec((tm, tn), lambda i, j: (i, j)),
            pl.BlockSpec((tm, LANES), lambda i, j: (i, 0)),
        ],
        out_shape=[jax.ShapeDtypeStruct((t, n), F32), jax.ShapeDtypeStruct((t, LANES), F32)],
        scratch_shapes=[pltpu.VMEM((tm, d), BF16)],
        compiler_params=_params("parallel", "arbitrary"),
        name="in_proj",
    )(x2d, g, w_main, w_a)


def _conv_kernel(a1_ref, a2_ref, p1_ref, p2_ref, m1_ref, m2_ref, cw_ref, cb_ref, lg_ref, lb_ref,
                 wo_ref, gm_ref, o_ref, hbuf_ref, cbuf_ref, hn_ref):
    tc, ch = a1_ref.shape
    first_tile = pl.program_id(1) == 0

    for r0 in range(0, tc, 64):
        hbuf_ref[CONV_HALO + r0:CONV_HALO + r0 + 64, :] = (
            a1_ref[r0:r0 + 64, :] * _sigmoid(a2_ref[r0:r0 + 64, :]))

    @pl.when(first_tile)
    def _():
        hbuf_ref[0:CONV_HALO, :] = m1_ref[...] * _sigmoid(m2_ref[...])

    @pl.when(jnp.logical_not(first_tile))
    def _():
        hbuf_ref[0:CONV_HALO, :] = p1_ref[...] * _sigmoid(p2_ref[...])

    sub = 8
    rb = 128
    for c0 in range(0, ch, LANES):
        lanes = slice(c0, c0 + LANES)
        for t0 in range(0, tc, rb):
            acc = jnp.zeros((rb, LANES), F32) + cb_ref[:, lanes]
            for s in range(sub):
                lead = 0 if s == 0 else sub
                part = None
                for a in range((CONV_WIDTH - 1 - s) // sub + 1):
                    j = CONV_WIDTH - 1 - (sub * a + s)
                    r0 = CONV_HALO + t0 - lead - sub * a
                    term = hbuf_ref[r0:r0 + rb + lead, lanes] * cw_ref[j:j + 1, lanes]
                    part = term if part is None else part + term
                acc = acc + part[lead - s:lead - s + rb, :]
            cbuf_ref[t0:t0 + rb, lanes] = acc

    rows = 32
    def ln_body(r, _):
        r0 = pl.multiple_of(r * rows, rows)
        x = cbuf_ref[pl.ds(r0, rows), :]
        mu = jnp.mean(x, axis=-1, keepdims=True)
        xc = x - mu
        var = jnp.mean(xc * xc, axis=-1, keepdims=True)
        y = xc * lax.rsqrt(var + EPS) * lg_ref[...] + lb_ref[...]
        hn_ref[pl.ds(r0, rows), :] = (y * _sigmoid(y)).astype(BF16)
        return 0
    lax.fori_loop(0, tc // rows, ln_body, 0)

    y = jnp.dot(hn_ref[...], wo_ref[...], preferred_element_type=F32)
    o_ref[...] = y * _sigmoid(gm_ref[...])


def _conv_module(proj, proj_m, conv_w, conv_b, ln_g, ln_b, w_out, nb, ls, ch, tc):
    nt = ls // tc
    hpt = tc // CONV_HALO
    meta_blk = META_TILE // CONV_HALO - 1
    merge_blk = _proj_col_starts(ch)["gate_conv"] // ch

    def cur(col):
        return lambda b, i: (b * nt + i, col)

    def prev(col):
        return lambda b, i: (jnp.maximum((b * nt + i) * hpt - 1, 0), col)

    const = lambda b, i: (0, 0)
    return pl.pallas_call(
        _conv_kernel,
        grid=(nb, nt),
        in_specs=[
            pl.BlockSpec((tc, ch), cur(0)),
            pl.BlockSpec((tc, ch), cur(1)),
            pl.BlockSpec((CONV_HALO, ch), prev(0)),
            pl.BlockSpec((CONV_HALO, ch), prev(1)),
            pl.BlockSpec((CONV_HALO, ch), lambda b, i: (meta_blk, 0)),
            pl.BlockSpec((CONV_HALO, ch), lambda b, i: (meta_blk, 1)),
            pl.BlockSpec((CONV_HALO, ch), const),
            pl.BlockSpec((1, ch), const),
            pl.BlockSpec((1, ch), const),
            pl.BlockSpec((1, ch), const),
            pl.BlockSpec((ch, ch), const, pipeline_mode=pl.Buffered(1)),
            pl.BlockSpec((tc, ch), cur(merge_blk)),
        ],
        out_specs=pl.BlockSpec((tc, ch), cur(0)),
        out_shape=jax.ShapeDtypeStruct((nb * ls, ch), F32),
        scratch_shapes=[
            pltpu.VMEM((CONV_HALO + tc, ch), F32),
            pltpu.VMEM((tc, ch), F32),
            pltpu.VMEM((tc, ch), BF16),
        ],
        compiler_params=_params("parallel", "arbitrary"),
        name="conv_module",
    )(proj, proj, proj, proj, proj_m, proj_m, conv_w, conv_b, ln_g, ln_b, w_out, proj)


def _gla_kernel(q_ref, k_ref, v_ref, r_ref, gm_ref, a_ref, km_ref, vm_ref, am_ref, wa2_ref, ba_ref,
                ng_ref, wo_ref, o_ref, st_ref, ob_ref):
    nh = GLA_HEADS
    c = GLA_CHUNK
    tb = q_ref.shape[0]
    dk = q_ref.shape[1] // nh
    dv = v_ref.shape[1] // nh
    scale = dk ** -0.5

    row = lax.broadcasted_iota(jnp.int32, (c, c), 0)
    col = lax.broadcasted_iota(jnp.int32, (c, c), 1)
    tri = (col <= row).astype(F32)

    def log_decay(a_c):
        z = jnp.dot(a_c.astype(BF16), wa2_ref[...], preferred_element_type=F32) + ba_ref[...]
        return (jnp.minimum(z, 0.0) - jnp.log1p(jnp.exp(-jnp.abs(z)))) * (1.0 / GLA_TAU)

    def cumsum_rows(g):
        return jnp.dot(tri, g, preferred_element_type=F32, precision=lax.Precision.HIGHEST)

    @pl.when(pl.program_id(1) == 0)
    def _():
        g = log_decay(am_ref[...])
        ridx = lax.broadcasted_iota(jnp.int32, (c, 1), 0)
        g = jnp.where(ridx >= c - N_META, g, 0.0)
        b = cumsum_rows(g)
        bl = b[c - 1:c, :]
        for h in range(nh):
            ks = slice(h * dk, (h + 1) * dk)
            ke = km_ref[:, ks] * jnp.exp(bl[:, ks] - b[:, ks])
            vh = vm_ref[:, h * dv:(h + 1) * dv]
            st_ref[h] = lax.dot_general(vh.astype(BF16), ke.astype(BF16), _TN, preferred_element_type=F32)

    levels = (32, 16, 8)
    level_masks = []
    for s in levels:
        level_masks.append((row // (2 * s) == col // (2 * s)) & ((row // s) % 2 == 1) & ((col // s) % 2 == 0))
    sub = lax.broadcasted_iota(jnp.int32, (8, c), 0)
    lane = lax.broadcasted_iota(jnp.int32, (8, c), 1)

    def chunk_body(ci, carry):
        r0 = pl.multiple_of(ci * c, c)
        g_all = log_decay(a_ref[pl.ds(r0, c), :])
        b_all = cumsum_rows(g_all)
        for h in range(nh):
            ks = slice(h * dk, (h + 1) * dk)
            vs = slice(h * dv, (h + 1) * dv)
            b = b_all[:, ks]
            q = q_ref[pl.ds(r0, c), ks] * scale
            k = k_ref[pl.ds(r0, c), ks]
            vb = v_ref[pl.ds(r0, c), vs].astype(BF16)
            bl = b[c - 1:c, :]
            st = st_ref[h]

            o = lax.dot_general((q * jnp.exp(b)).astype(BF16), st.astype(BF16), _NT,
                                preferred_element_type=F32)

            att = jnp.zeros((c, c), F32)
            for s, msk in zip(levels, level_masks):
                pieces = []
                for p in range(c // (2 * s)):
                    rr = p * 2 * s + s - 1
                    pieces.append(jnp.broadcast_to(b[rr:rr + 1, :], (2 * s, dk)))
                bref = pieces[0] if len(pieces) == 1 else jnp.concatenate(pieces, axis=0)
                qs = q * jnp.exp(jnp.minimum(b - bref, 0.0))
                kk = k * jnp.exp(jnp.minimum(bref - b, 0.0))
                pm = lax.dot_general(qs.astype(BF16), kk.astype(BF16), _NT, preferred_element_type=F32)
                att = att + jnp.where(msk, pm, 0.0)

            blocks = []
            for bi in range(c // 8):
                qb = q[bi * 8:(bi + 1) * 8, :]
                bb = b[bi * 8:(bi + 1) * 8, :]
                blk = jnp.zeros((8, c), F32)
                for jj in range(8):
                    j = bi * 8 + jj
                    e = jnp.exp(jnp.minimum(bb - b[j:j + 1, :], 0.0))
                    colv = jnp.sum(qb * (k[j:j + 1, :] * e), axis=-1, keepdims=True)
                    blk = blk + jnp.where((lane == j) & (sub >= jj), colv, 0.0)
                blocks.append(blk)
            att = att + jnp.concatenate(blocks, axis=0)

            o = o + jnp.dot(att.astype(BF16), vb, preferred_element_type=F32)

            ke = k * jnp.exp(bl - b)
            st_ref[h] = st * jnp.exp(bl) + lax.dot_general(vb, ke.astype(BF16), _TN,
                                                            preferred_element_type=F32)

            ms = jnp.mean(o * o, axis=-1, keepdims=True)
            rr_ = r_ref[pl.ds(r0, c), vs]
            on = o * lax.rsqrt(ms + EPS) * ng_ref[:, vs] * (rr_ * _sigmoid(rr_))
            ob_ref[pl.ds(r0, c), vs] = on.astype(BF16)
        return carry

    lax.fori_loop(0, tb // c, chunk_body, 0)

    y = jnp.dot(ob_ref[...], wo_ref[...], preferred_element_type=F32)
    o_ref[...] = y * _sigmoid(gm_ref[...])


def _gla_module(proj, a_x, proj_m, a_m, wa2, ba, ng, w_out, nb, ls, d, tb):
    nt = ls // tb
    nqk = d // 2
    c = GLA_CHUNK
    mrow = META_TILE // c - 1
    cols = _proj_col_starts(d)

    def cur(col):
        return lambda b, i: (b * nt + i, col)

    const = lambda b, i: (0, 0)
    return pl.pallas_call(
        _gla_kernel,
        grid=(nb, nt),
        in_specs=[
            pl.BlockSpec((tb, nqk), cur(cols["q"] // nqk)),
            pl.BlockSpec((tb, nqk), cur(cols["k"] // nqk)),
            pl.BlockSpec((tb, d), cur(cols["v"] // d)),
            pl.BlockSpec((tb, d), cur(cols["r"] // d)),
            pl.BlockSpec((tb, d), cur(cols["gate_gla"] // d)),
            pl.BlockSpec((tb, LANES), cur(0)),
            pl.BlockSpec((c, nqk), lambda b, i: (mrow, cols["k"] // nqk)),
            pl.BlockSpec((c, d), lambda b, i: (mrow, cols["v"] // d)),
            pl.BlockSpec((c, LANES), lambda b, i: (mrow, 0)),
            pl.BlockSpec((LANES, nqk), const),
            pl.BlockSpec((1, nqk), const),
            pl.BlockSpec((1, d), const),
            pl.BlockSpec((d, d), const, pipeline_mode=pl.Buffered(1)),
        ],
        out_specs=pl.BlockSpec((tb, d), cur(0)),
        out_shape=jax.ShapeDtypeStruct((nb * ls, d), F32),
        scratch_shapes=[
            pltpu.VMEM((GLA_HEADS, d // GLA_HEADS, nqk // GLA_HEADS), F32),
            pltpu.VMEM((tb, d), BF16),
        ],
        compiler_params=_params("parallel", "arbitrary"),
        name="gla_module",
    )(proj, proj, proj, proj, proj, a_x, proj_m, proj_m, a_m, wa2, ba, ng, w_out)


def _mix_kernel(yc_ref, yg_ref, x_ref, wo_ref, g2_ref, wq_ref, h1_ref, xn2_ref, q2_ref):
    m = (yc_ref[...] + yg_ref[...]).astype(BF16)
    h1 = x_ref[...] + jnp.dot(m, wo_ref[...], preferred_element_type=F32)
    h1_ref[...] = h1
    ms = jnp.mean(h1 * h1, axis=-1, keepdims=True)
    xn2 = (h1 * lax.rsqrt(ms + EPS) * g2_ref[...]).astype(BF16)
    xn2_ref[...] = xn2
    q2_ref[...] = jnp.dot(xn2, wq_ref[...], preferred_element_type=F32).astype(BF16)


def _mix_module(yc, yg, x2d, w_out, g2, wq, tm):
    t, d = x2d.shape
    row = lambda i: (i, 0)
    const = lambda i: (0, 0)
    return pl.pallas_call(
        _mix_kernel,
        grid=(t // tm,),
        in_specs=[
            pl.BlockSpec((tm, d), row), pl.BlockSpec((tm, d), row), pl.BlockSpec((tm, d), row),
            pl.BlockSpec((d, d), const), pl.BlockSpec((1, d), const), pl.BlockSpec((d, d), const),
        ],
        out_specs=[pl.BlockSpec((tm, d), row), pl.BlockSpec((tm, d), row), pl.BlockSpec((tm, d), row)],
        out_shape=[jax.ShapeDtypeStruct((t, d), F32), jax.ShapeDtypeStruct((t, d), BF16),
                   jax.ShapeDtypeStruct((t, d), BF16)],
        compiler_params=_params("parallel"),
        name="mix_module",
    )(yc, yg, x2d, w_out, g2, wq)


_NO_RANK = 100.0


def _top16(s):
    idx = lax.broadcasted_iota(jnp.int32, s.shape, 0).astype(F32)
    rank = jnp.full(s.shape, _NO_RANK, F32)
    vals = []
    for r in range(PEER_TOPK):
        m = jnp.max(s, axis=0, keepdims=True)
        first = jnp.min(jnp.where(s == m, idx, 1e9), axis=0, keepdims=True)
        sel = idx == first
        rank = jnp.where(sel, float(r), rank)
        s = jnp.where(sel, -jnp.inf, s)
        vals.append(m)
    return rank, jnp.concatenate(vals, axis=0)


def _top16_distinct(s):
    rank = jnp.full(s.shape, _NO_RANK, F32)
    vals = []
    for r in range(PEER_TOPK):
        m = jnp.max(s, axis=0, keepdims=True)
        sel = s == m
        rank = jnp.where(sel, float(r), rank)
        s = jnp.where(sel, -jnp.inf, s)
        vals.append(m)
    n_ranked = jnp.sum(jnp.where(rank < float(PEER_TOPK), 1.0, 0.0), axis=0, keepdims=True)
    return rank, jnp.concatenate(vals, axis=0), n_ranked


def _frontier_counts(v1, v2):
    kk = PEER_TOPK
    r1 = lax.broadcasted_iota(jnp.int32, v1.shape, 0).astype(F32)
    cnt = jnp.zeros(v1.shape, F32)
    front = v1 + v2[0:1, :]
    for it in range(kk):
        m = jnp.max(front, axis=0, keepdims=True)
        first = jnp.min(jnp.where(front == m, r1, 99.0), axis=0, keepdims=True)
        sel = r1 == first
        cnt = cnt + sel.astype(F32)
        if it + 1 < kk:
            c_sel = jnp.sum(jnp.where(sel, cnt, 0.0), axis=0, keepdims=True)
            nxt = jnp.sum(jnp.where(r1 == c_sel, v2, 0.0), axis=0, keepdims=True)
            nxt = jnp.where(c_sel >= float(kk), -jnp.inf, nxt)
            front = jnp.where(sel, v1 + nxt, front)
    return cnt


def _peer_kernel(xn_ref, q_ref, k1_ref, k2_ref, u_ref, vt_ref, h1_ref, gf_ref, o_ref,
                 yt_ref, hta_ref, htb_ref, w_ref, p1_ref, c1_ref, r2_ref, p2_ref):
    e = pl.program_id(1)
    n_steps = pl.num_programs(1)
    tt = xn_ref.shape[0]
    te = u_ref.shape[0]
    nk = PEER_NKEYS
    kk = PEER_TOPK
    n_i1 = te // nk
    pack = 16

    def expert_acts():
        return lax.dot_general(u_ref[...], xn_ref[...], _NT, preferred_element_type=F32)

    s1_ref = htb_ref.at[0:nk]
    s2_ref = htb_ref.at[nk:2 * nk]

    @pl.when(e == 0)
    def _():
        yt_ref[...] = jnp.zeros_like(yt_ref)
        hta_ref[...] = expert_acts()

        def head_body(h, carry):
            c0 = pl.multiple_of(h * 2 * nk, 2 * nk)
            s1_ref[...] = lax.dot_general(k1_ref[h], q_ref[:, pl.ds(c0, nk)], _NT,
                                          preferred_element_type=F32)
            s2_ref[...] = lax.dot_general(k2_ref[h], q_ref[:, pl.ds(c0 + nk, nk)], _NT,
                                          preferred_element_type=F32)

            def lane_body(lt, carry2):
                l0 = pl.multiple_of(lt * LANES, LANES)
                s1 = s1_ref[:, pl.ds(l0, LANES)]
                s2 = s2_ref[:, pl.ds(l0, LANES)]

                def route(rank1, v1, rank2, v2):
                    cnt = _frontier_counts(v1, v2)
                    p1v = jnp.exp(v1 - v1[0:1, :])
                    p2v = jnp.exp(v2 - v2[0:1, :])
                    cum = jnp.zeros(v1.shape, F32)
                    for r2 in range(kk):
                        cum = cum + jnp.where(cnt > float(r2), p2v[r2:r2 + 1, :], 0.0)
                    z = jnp.sum(p1v * cum, axis=0, keepdims=True)
                    c1 = jnp.zeros(s1.shape, F32)
                    for r in range(kk):
                        c1 = jnp.where(rank1 == float(r), cnt[r:r + 1, :], c1)
                    p1_ref[h, :, pl.ds(l0, LANES)] = jnp.exp(s1 - v1[0:1, :]) / z
                    c1_ref[h, :, pl.ds(l0, LANES)] = c1
                    r2_ref[h, :, pl.ds(l0, LANES)] = rank2.astype(BF16)
                    p2_ref[h, :, pl.ds(l0, LANES)] = jnp.exp(s2 - v2[0:1, :]).astype(BF16)

                rank1, v1, n1 = _top16_distinct(s1)
                rank2, v2, n2 = _top16_distinct(s2)
                route(rank1, v1, rank2, v2)
                ties = jnp.max(jnp.abs(n1 - float(kk)) + jnp.abs(n2 - float(kk)))

                @pl.when(ties > 0.0)
                def _():
                    route(*_top16(s1), *_top16(s2))

                return carry2

            lax.fori_loop(0, tt // LANES, lane_body, 0)
            return carry

        lax.fori_loop(0, PEER_HEADS, head_body, 0)

    def main_step(ht_w, ht_r):
        ht_w[...] = expert_acts()
        i1_base = pl.multiple_of((e - 1) * n_i1, n_i1)
        for lt in range(tt // LANES):
            ls_ = slice(lt * LANES, (lt + 1) * LANES)
            c1s = [c1_ref[h, pl.ds(i1_base, n_i1), ls_] for h in range(PEER_HEADS)]
            p1s = [p1_ref[h, pl.ds(i1_base, n_i1), ls_] for h in range(PEER_HEADS)]
            for ii in range(n_i1):
                rows = slice(ii * nk, (ii + 1) * nk)
                c1b = [jnp.broadcast_to(c[ii:ii + 1, :], (pack, LANES)).astype(BF16) for c in c1s]
                p1b = [jnp.broadcast_to(p[ii:ii + 1, :], (pack, LANES)).astype(BF16) for p in p1s]
                x = ht_r[rows, ls_]
                act = (0.5 * x * (1.0 + lax.erf(x * (2.0 ** -0.5)))).astype(BF16)
                for rg in range(nk // pack):
                    rs = slice(rg * pack, (rg + 1) * pack)
                    g = jnp.zeros((pack, LANES), BF16)
                    for h in range(PEER_HEADS):
                        m = jnp.minimum(jnp.maximum(c1b[h] - r2_ref[h, rs, ls_], jnp.zeros((), BF16)), p1b[h])
                        g = g + m * p2_ref[h, rs, ls_]
                    w_ref[ii * nk + rg * pack:ii * nk + (rg + 1) * pack, ls_] = act[rs, :] * g
        yt_ref[...] += jnp.dot(vt_ref[...], w_ref[...], preferred_element_type=F32)

    odd = lax.rem(e, 2) == 1

    @pl.when(odd)
    def _():
        main_step(htb_ref, hta_ref)

    @pl.when(jnp.logical_and(e > 0, jnp.logical_not(odd)))
    def _():
        main_step(hta_ref, htb_ref)

    @pl.when(e == n_steps - 1)
    def _():
        for r0 in range(0, tt, LANES):
            h = h1_ref[r0:r0 + LANES, :] + yt_ref[:, r0:r0 + LANES].T
            ms = jnp.mean(h * h, axis=-1, keepdims=True)
            o_ref[r0:r0 + LANES, :] = h * lax.rsqrt(ms + EPS) * gf_ref[...]


def _peer_module(xn2, q2, k1, k2, u, vt, h1, gf, tt, te):
    t, d = xn2.shape
    ne = u.shape[0] // te
    nk = PEER_NKEYS
    sel32 = pltpu.VMEM((PEER_HEADS, nk, tt), F32)
    sel16 = pltpu.VMEM((PEER_HEADS, nk, tt), BF16)
    return pl.pallas_call(
        _peer_kernel,
        grid=(t // tt, ne + 1),
        in_specs=[
            pl.BlockSpec((tt, d), lambda i, e: (i, 0)),
            pl.BlockSpec((tt, d), lambda i, e: (i, 0), pipeline_mode=pl.Buffered(1)),
            pl.BlockSpec((PEER_HEADS, nk, nk), lambda i, e: (0, 0, 0)),
            pl.BlockSpec((PEER_HEADS, nk, nk), lambda i, e: (0, 0, 0)),
            pl.BlockSpec((te, d), lambda i, e: (jnp.minimum(e, ne - 1), 0)),
            pl.BlockSpec((d, te), lambda i, e: (0, jnp.maximum(e - 1, 0))),
            pl.BlockSpec((tt, d), lambda i, e: (i, 0)),
            pl.BlockSpec((1, d), lambda i, e: (0, 0)),
        ],
        out_specs=pl.BlockSpec((tt, d), lambda i, e: (i, 0)),
        out_shape=jax.ShapeDtypeStruct((t, d), F32),
        scratch_shapes=[
            pltpu.VMEM((d, tt), F32),
            pltpu.VMEM((te, tt), F32), pltpu.VMEM((te, tt), F32),
            pltpu.VMEM((te, tt), BF16),
            sel32, sel32, sel16, sel16,
        ],
        compiler_params=_params("parallel", "arbitrary"),
        name="peer_module",
    )(xn2, q2, k1, k2, u, vt, h1, gf)


def _layer(x2d, xm, nb, ls, norm1_g, w_in, w_alpha2, b_alpha, conv_w, conv_b, conv_ln_g, conv_ln_b,
           w_conv_out, gla_norm_g, w_gla_out, w_out, norm2_g, peer_wq, peer_k1, peer_k2, peer_u, peer_v, normf_g):
    d = x2d.shape[1]
    n_pre = 5 * d
    rank = w_alpha2.shape[0]
    w_bf = w_in.astype(BF16)
    w_main = jnp.concatenate([w_bf[:, :n_pre], w_bf[:, n_pre + rank:]], axis=1)
    w_a = jnp.pad(w_bf[:, n_pre:n_pre + rank], ((0, 0), (0, LANES - rank)))
    g1 = norm1_g.reshape(1, d)

    tm = min(INPROJ_TM, x2d.shape[0])
    proj, a_x = _in_proj(x2d, g1, w_main, w_a, tm, INPROJ_TN)
    proj_m, a_m = _in_proj(xm, g1, w_main, w_a, META_TILE, INPROJ_TN)

    cw = jnp.pad(conv_w, ((0, CONV_HALO - CONV_WIDTH), (0, 0)))
    yc = _conv_module(proj, proj_m, cw, conv_b.reshape(1, d), conv_ln_g.reshape(1, d), conv_ln_b.reshape(1, d),
                      w_conv_out.astype(BF16), nb, ls, d, min(SEQ_TILE, ls))

    wa2 = jnp.pad(w_alpha2, ((0, LANES - rank), (0, 0))).astype(BF16)
    yg = _gla_module(proj, a_x, proj_m, a_m, wa2, b_alpha.reshape(1, -1), gla_norm_g.reshape(1, d),
                     w_gla_out.astype(BF16), nb, ls, d, min(SEQ_TILE, ls))

    h1, xn2, q2 = _mix_module(yc, yg, x2d, w_out.astype(BF16), norm2_g.reshape(1, d), peer_wq.astype(BF16),
                              MIX_TM)

    return _peer_module(xn2, q2, peer_k1.astype(BF16), peer_k2.astype(BF16), peer_u.astype(BF16),
                        peer_v.astype(BF16).T, h1, normf_g.reshape(1, d), min(PEER_TT, x2d.shape[0]), PEER_TE)


def kernel(x, meta_tokens, norm1_g, w_in, w_alpha2, b_alpha, conv_w, conv_b, conv_ln_g, conv_ln_b, w_conv_out,
           gla_norm_g, w_gla_out, w_out, norm2_g, peer_wq, peer_k1, peer_k2, peer_u, peer_v, normf_g):
    nb, ls, d = x.shape
    depth = w_in.shape[0]
    assert depth == 1, "meta-token positions are only carried through one layer"
    x2d = x.reshape(nb * ls, d)
    xm = jnp.concatenate([jnp.zeros((META_TILE - N_META, d), x.dtype), meta_tokens.astype(x.dtype)], axis=0)
    out = _layer(x2d, xm, nb, ls, norm1_g[0], w_in[0], w_alpha2[0], b_alpha[0], conv_w[0], conv_b[0],
                 conv_ln_g[0], conv_ln_b[0], w_conv_out[0], gla_norm_g[0], w_gla_out[0], w_out[0],
                 norm2_g[0], peer_wq[0], peer_k1[0], peer_k2[0], peer_u[0], peer_v[0], normf_g)
    return out.reshape(nb, ls, d)
```
